```python
import math
import jax
import jax.numpy as jnp
from jax import lax
import numpy as np

D_MODEL = 1024
BATCH = 16
SEQ = 2048
DEPTH = 4

CTX_LEN = 256
GRID_W = 64
HEAD_DIM = 64
NA_HEADS = D_MODEL // (2 * HEAD_DIM)
NA_KH = 8
NA_KW = 16
DIFF_HEADS = D_MODEL // (4 * HEAD_DIM)
DIFF_VDIM = 2 * HEAD_DIM
GQA_Q_HEADS = D_MODEL // HEAD_DIM
GQA_KV_HEADS = 4
GQA_GROUP = GQA_Q_HEADS // GQA_KV_HEADS
D_FF = 2816
CONV_WIDTH = 3
Q_BLOCK = 128
ROPE_BASE = 10000.0
EPS = 1e-6
N_EVEN = (DEPTH + 1) // 2
N_ODD = DEPTH // 2
ALPHA = (2 * DEPTH) ** 0.25
BETA = (8 * DEPTH) ** -0.25
NA_W = NA_HEADS * HEAD_DIM
DIFF_W = DIFF_HEADS * 2 * HEAD_DIM
GQA_QW = GQA_Q_HEADS * HEAD_DIM
GQA_KW = GQA_KV_HEADS * HEAD_DIM

kernel_name = 'hybrid_natten_diff_gqa_convffn_dit'


def layer_norm(x, g, b):
    xf = x.astype(jnp.float32)
    mu = jnp.mean(xf, -1, keepdims=True)
    var = jnp.mean(jnp.square(xf - mu), -1, keepdims=True)
    return ((xf - mu) * lax.rsqrt(var + EPS)).astype(x.dtype) * g + b


def rms_norm(x, g):
    xf = x.astype(jnp.float32)
    return (xf * lax.rsqrt(jnp.mean(xf * xf, -1, keepdims=True) + EPS)).astype(x.dtype) * g


def softmax_f32(s):
    return jax.nn.softmax(s.astype(jnp.float32), axis=-1)


def modulate(x, shift, scale):
    return x * (1.0 + scale) + shift


def heads(t, n):
    b, t_len, _ = t.shape
    return t.reshape(b, t_len, n, -1).transpose(0, 2, 1, 3)


def merge_heads(t):
    b, n, t_len, d = t.shape
    return t.transpose(0, 2, 1, 3).reshape(b, t_len, n * d)


def rope_2d_tables(n_tokens, dtype):
    n_freq = HEAD_DIM // 4
    inv = ROPE_BASE ** (-jnp.arange(n_freq, dtype=jnp.float32) / n_freq)
    t = jnp.arange(n_tokens)
    row = (t // GRID_W).astype(jnp.float32)
    col = (t % GRID_W).astype(jnp.float32)
    ar = row[:, None] * inv
    ac = col[:, None] * inv
    ang = jnp.concatenate([ar, ar, ac, ac], -1)
    return jnp.cos(ang).astype(dtype), jnp.sin(ang).astype(dtype)


def apply_rope_2d(x, cos, sin):
    x1, x2, x3, x4 = jnp.split(x, 4, axis=-1)
    rot = jnp.concatenate([-x2, x1, -x4, x3], -1)
    return x * cos + rot * sin


def block_attention(q, k, v, scale):
    b, hk, g, s, d = q.shape
    nb = s // Q_BLOCK
    qb = jnp.moveaxis(q.reshape(b, hk, g, nb, Q_BLOCK, d), 3, 0)

    def one_block(qi):
        p = softmax_f32(jnp.einsum('bhgqd,bhkd->bhgqk', qi, k).astype(jnp.float32) * scale)
        return jnp.einsum('bhgqk,bhkd->bhgqd', p.astype(v.dtype), v)

    out = lax.map(one_block, qb)
    return jnp.moveaxis(out, 0, 3).reshape(b, hk, g, s, v.shape[-1])


def diff_block_attention(q1, q2, k1, k2, v, lam, scale):
    b, h, s, d = q1.shape
    nb = s // Q_BLOCK

    def to_blocks(t):
        return jnp.moveaxis(t.reshape(b, h, nb, Q_BLOCK, d), 2, 0)

    def one_block(qs):
        a1, a2 = qs
        p1 = softmax_f32(jnp.einsum('bhqd,bhkd->bhqk', a1, k1).astype(jnp.float32) * scale)
        p2 = softmax_f32(jnp.einsum('bhqd,bhkd->bhqk', a2, k2).astype(jnp.float32) * scale)
        return jnp.einsum('bhqk,bhkd->bhqd', (p1 - lam * p2).astype(v.dtype), v)

    out = lax.map(one_block, (to_blocks(q1), to_blocks(q2)))
    return jnp.moveaxis(out, 0, 2).reshape(b, h, s, v.shape[-1])


def neighbourhood_attention(q, k, v, kc, vc, rpb, scale):
    b, h, s, d = q.shape
    rows = s // GRID_W
    kh = min(NA_KH, rows)
    r = jnp.arange(rows)
    c = jnp.arange(GRID_W)
    row_start = jnp.clip(r - kh // 2, 0, rows - kh)
    col_start = jnp.clip(c - NA_KW // 2, 0, GRID_W - NA_KW)
    col_ok = (c[None, :] >= col_start[:, None]) & (c[None, :] < col_start[:, None] + NA_KW)
    dc_idx = jnp.clip(c[None, :] - c[:, None] + NA_KW - 1, 0, 2 * NA_KW - 2)
    bias_col = jnp.where(col_ok, rpb[:, :, dc_idx].astype(jnp.float32), -jnp.inf)
    k_grid = k.reshape(b, h, rows, GRID_W, d)
    v_grid = v.reshape(b, h, rows, GRID_W, d)
    q_rows = jnp.moveaxis(q.reshape(b, h, rows, GRID_W, d), 2, 0)
    n_loc = kh * GRID_W

    def one_row(args):
        qr, ri, rs = args
        kr = lax.dynamic_slice_in_dim(k_grid, rs, kh, axis=2)
        vr = lax.dynamic_slice_in_dim(v_grid, rs, kh, axis=2)
        bias = bias_col[:, rs + jnp.arange(kh) - ri + NA_KH - 1]
        s_loc = jnp.einsum('bhqd,bhikd->bhqik', qr, kr).astype(jnp.float32) * scale \
            + jnp.transpose(bias, (0, 2, 1, 3))[None]
        s_ctx = jnp.einsum('bhqd,bhld->bhql', qr, kc).astype(jnp.float32) * scale
        p = softmax_f32(jnp.concatenate([s_loc.reshape(b, h, GRID_W, n_loc), s_ctx], -1)).astype(v.dtype)
        return (jnp.einsum('bhqk,bhkd->bhqd', p[..., :n_loc], vr.reshape(b, h, n_loc, d))
                + jnp.einsum('bhql,bhld->bhqd', p[..., n_loc:], vc))

    out = lax.map(one_row, (q_rows, r, row_start))
    return jnp.moveaxis(out, 0, 2).reshape(b, h, s, d)


def mixer_ab(h, hc, w_in, w_o, rpb, lam_vec, subln_g, lambda_init, rope, need_ctx):
    cos, sin = rope
    scale = HEAD_DIM ** -0.5
    lv = lam_vec.astype(jnp.float32)
    lam = jnp.exp(jnp.sum(lv[0] * lv[1])) - jnp.exp(jnp.sum(lv[2] * lv[3])) + lambda_init
    splits = [NA_W, 2 * NA_W, 3 * NA_W, 3 * NA_W + DIFF_W, 3 * NA_W + 2 * DIFF_W]

    def project(t):
        nq, nk, nv, dq, dk, dv = jnp.split(t @ w_in, splits, axis=-1)
        return (heads(nq, NA_HEADS), heads(nk, NA_HEADS), heads(nv, NA_HEADS),
                heads(dq, 2 * DIFF_HEADS), heads(dk, 2 * DIFF_HEADS), heads(dv, DIFF_HEADS))

    def diff_post(o):
        return rms_norm(o, subln_g) * (1.0 - lambda_init)

    def out_proj(na_o, diff_o):
        return jnp.concatenate([merge_heads(na_o), merge_heads(diff_o)], -1) @ w_o

    nq, nk, nv, dq, dk, dv = project(h)
    cnq, cnk, cnv, cdq, cdk, cdv = project(hc)
    na = neighbourhood_attention(nq, nk, nv, cnk, cnv, rpb, scale)
    dq = apply_rope_2d(dq, cos, sin)
    dk = apply_rope_2d(dk, cos, sin)
    k1 = jnp.concatenate([dk[:, 0::2], cdk[:, 0::2]], axis=2)
    k2 = jnp.concatenate([dk[:, 1::2], cdk[:, 1::2]], axis=2)
    v_all = jnp.concatenate([dv, cdv], axis=2)
    diff = diff_post(diff_block_attention(dq[:, 0::2], dq[:, 1::2], k1, k2, v_all, lam, scale))
    y = out_proj(na, diff)
    if not need_ctx:
        return y, None
    na_c = block_attention(cnq[:, :, None], cnk, cnv, scale)[:, :, 0]
    diff_c = diff_post(diff_block_attention(cdq[:, 0::2], cdq[:, 1::2], cdk[:, 0::2], cdk[:, 1::2], cdv, lam, scale))
    return y, out_proj(na_c, diff_c)


def mixer_c(h, hc, w_in, w_o, qk_g, rope, need_ctx):
    cos, sin = rope
    scale = HEAD_DIM ** -0.5

    def project(t):
        b, t_len, _ = t.shape
        q, k, v = jnp.split(t @ w_in, [GQA_QW, GQA_QW + GQA_KW], axis=-1)
        q = rms_norm(heads(q, GQA_Q_HEADS), qk_g[0]).reshape(b, GQA_KV_HEADS, GQA_GROUP, t_len, HEAD_DIM)
        k = rms_norm(heads(k, GQA_KV_HEADS), qk_g[1])
        return q, k, heads(v, GQA_KV_HEADS)

    def out_proj(o):
        b, _, _, t_len, _ = o.shape
        return merge_heads(o.reshape(b, GQA_Q_HEADS, t_len, HEAD_DIM)) @ w_o

    q, k, v = project(h)
    cq, ck, cv = project(hc)
    q = apply_rope_2d(q, cos, sin)
    k = apply_rope_2d(k, cos, sin)
    y = out_proj(block_attention(q, jnp.concatenate([k, ck], 2), jnp.concatenate([v, cv], 2), scale))
    if not need_ctx:
        return y, None
    return y, out_proj(block_attention(cq, ck, cv, scale))


def conv_ffn(h, w_up, conv_w, conv_b, w_down):
    t_len = h.shape[1]
    pad = CONV_WIDTH // 2
    up = jnp.pad(h @ w_up, ((0, 0), (pad, pad), (0, 0)))
    u = sum(up[:, j:j + t_len] * conv_w[j] for j in range(CONV_WIDTH)) + conv_b
    a, g = jnp.split(u, 2, axis=-1)
    return (jax.nn.gelu(g) * a) @ w_down


def setup_inputs(seed: int = 0) -> dict:
    key = jax.random.key(seed)
    ks = jax.random.split(key, 32)
    f32 = jnp.float32
    D = D_MODEL

    def nrm(k, shape, s):
        return jax.random.normal(k, shape, f32) * s

    sd = D ** -0.5
    return {
        'x': nrm(ks[0], (BATCH, SEQ, D), 1.0),
        'c': nrm(ks[1], (BATCH, D), 1.0),
        'ctx': nrm(ks[2], (BATCH, CTX_LEN, D), 1.0),
        'c_ctx': nrm(ks[3], (D,), 1.0),
        'w_ada': nrm(ks[4], (DEPTH, D, 6 * D), sd),
        'b_ada': nrm(ks[5], (DEPTH, 6 * D), 0.02),
        'ln_g': 1.0 + nrm(ks[6], (DEPTH, 2, D), 0.02),
        'ln_b': nrm(ks[7], (DEPTH, 2, D), 0.02),
        'w_in_ab': jnp.concatenate([
            nrm(ks[8], (N_EVEN, D, 2 * NA_W), sd),
            nrm(ks[9], (N_EVEN, D, NA_W), BETA * sd),
            nrm(ks[10], (N_EVEN, D, 2 * DIFF_W), sd),
            nrm(ks[11], (N_EVEN, D, DIFF_HEADS * DIFF_VDIM), BETA * sd)], axis=-1),
        'w_o_ab': nrm(ks[12], (N_EVEN, NA_W + DIFF_HEADS * DIFF_VDIM, D), BETA * sd),
        'na_rpb': nrm(ks[13], (N_EVEN, NA_HEADS, 2 * NA_KH - 1, 2 * NA_KW - 1), 0.1),
        'diff_lambda': nrm(ks[14], (N_EVEN, 4, HEAD_DIM), 0.1),
        'diff_subln': 1.0 + nrm(ks[15], (N_EVEN, DIFF_VDIM), 0.02),
        'w_in_c': jnp.concatenate([
            nrm(ks[16], (N_ODD, D, GQA_QW + GQA_KW), sd),
            nrm(ks[17], (N_ODD, D, GQA_KW), BETA * sd)], axis=-1),
        'w_o_c': nrm(ks[18], (N_ODD, GQA_QW, D), BETA * GQA_QW ** -0.5),
        'gqa_qk_norm': 1.0 + nrm(ks[19], (N_ODD, 2, HEAD_DIM), 0.02),
        'w_up': nrm(ks[20], (DEPTH, D, 2 * D_FF), BETA * sd),
        'conv_w': nrm(ks[21], (DEPTH, CONV_WIDTH, 2 * D_FF), CONV_WIDTH ** -0.5),
        'conv_b': nrm(ks[22], (DEPTH, 2 * D_FF), 0.02),
        'w_down': nrm(ks[23], (DEPTH, D_FF, D), BETA * D_FF ** -0.5),
    }


def reference(x, c, ctx, c_ctx, w_ada, b_ada, ln_g, ln_b, w_in_ab, w_o_ab, na_rpb, diff_lambda,
              diff_subln, w_in_c, w_o_c, gqa_qk_norm, w_up, conv_w, conv_b, w_down):
    rope = rope_2d_tables(x.shape[1], x.dtype)
    cond = jax.nn.silu(c)
    cond_ctx = jax.nn.silu(c_ctx)
    xc = ctx
    for l in range(DEPTH):
        need_ctx = l < DEPTH - 1
        mod = jnp.split((cond @ w_ada[l] + b_ada[l])[:, None, :], 6, axis=-1)
        mod_c = jnp.split(cond_ctx @ w_ada[l] + b_ada[l], 6, axis=-1)
        h = modulate(x, mod[0], mod[1])
        hc = modulate(xc, mod_c[0], mod_c[1])
        if l % 2 == 0:
            i = l // 2
            lambda_init = 0.8 - 0.6 * math.exp(-0.3 * l)
            y, yc = mixer_ab(h, hc, w_in_ab[i], w_o_ab[i], na_rpb[i], diff_lambda[i], diff_subln[i],
                             lambda_init, rope, need_ctx)
        else:
            i = l // 2
            y, yc = mixer_c(h, hc, w_in_c[i], w_o_c[i], gqa_qk_norm[i], rope, need_ctx)
        x = layer_norm(ALPHA * x + mod[2] * y, ln_g[l, 0], ln_b[l, 0])
        f = conv_ffn(modulate(x, mod[3], mod[4]), w_up[l], conv_w[l], conv_b[l], w_down[l])
        x = layer_norm(ALPHA * x + mod[5] * f, ln_g[l, 1], ln_b[l, 1])
        if need_ctx:
            xc = layer_norm(ALPHA * xc + mod_c[2] * yc, ln_g[l, 0], ln_b[l, 0])
            fc = conv_ffn(modulate(xc, mod_c[3], mod_c[4]), w_up[l], conv_w[l], conv_b[l], w_down[l])
            xc = layer_norm(ALPHA * xc + mod_c[5] * fc, ln_g[l, 1], ln_b[l, 1])
    return x
```

```python
import functools
import math

import numpy as np
import jax
import jax.numpy as jnp
from jax import lax
from jax.experimental import pallas as pl
from jax.experimental.pallas import tpu as pltpu

F32 = jnp.float32
BF16 = jnp.bfloat16

DEPTH = 4
GRID_W = 64
HEAD_DIM = 64
NA_KH = 8
NA_KW = 16
GQA_KV_HEADS = 4
CONV_WIDTH = 3
ROPE_BASE = 10000.0
EPS = 1e-6
ALPHA = (2 * DEPTH) ** 0.25
QK_SCALE = HEAD_DIM ** -0.5

LANES = 128
SUBLANES_F32 = 8
SUBLANES_BF16 = 16
MXU_DIM = 256
VMEM_LIMIT_BYTES = 56 * 1024 * 1024

PAIR = 2 * HEAD_DIM
assert PAIR == LANES

TM_LATENT = 512
TQ_GQA = 256
TQ_DIFF = 512
NA_ROWS_PER_BLOCK = 4
KEY_CHUNK = 512
FF_CHUNK = MXU_DIM
ADA_TN = 1536
HALO = SUBLANES_BF16


def _cparams():
    return pltpu.CompilerParams(vmem_limit_bytes=VMEM_LIMIT_BYTES)


def _const_spec(shape):
    nd = len(shape)
    return pl.BlockSpec(shape, lambda *_: (0,) * nd, pipeline_mode=pl.Buffered(1))


def _layer_norm(z, g, b):
    mu = jnp.mean(z, axis=-1, keepdims=True)
    d = z - mu
    var = jnp.mean(d * d, axis=-1, keepdims=True)
    return d * lax.rsqrt(var + EPS) * g + b


def _gelu_tanh(x):
    return 0.5 * x * (1.0 + jnp.tanh(math.sqrt(2.0 / math.pi) * (x + 0.044715 * (x * x * x))))


def _group_sum_matrix():
    r = lax.broadcasted_iota(jnp.int32, (2 * LANES, LANES), 0)
    c = lax.broadcasted_iota(jnp.int32, (2 * LANES, LANES), 1)
    same = ((r % LANES) // HEAD_DIM) == (c // HEAD_DIM)
    return jnp.where(same, 1.0, 0.0).astype(BF16)


def _head_rms(xb, gm, g):
    x2 = xb * xb
    hi = x2.astype(BF16)
    lo = (x2 - hi.astype(F32)).astype(BF16)
    ssq = jnp.dot(jnp.concatenate([hi, lo], axis=1), gm, preferred_element_type=F32)
    return xb * lax.rsqrt(ssq * (1.0 / HEAD_DIM) + EPS) * g


def _rope(xb, cos, sin_a, sin_b):
    return (xb * cos
            + pltpu.roll(xb, LANES - HEAD_DIM // 4, 1) * sin_a
            + pltpu.roll(xb, HEAD_DIM // 4, 1) * sin_b)


def _split_heads_rows(q_pair):
    qf = q_pair.astype(F32)
    lane = lax.broadcasted_iota(jnp.int32, qf.shape, 1)
    lo = lane < HEAD_DIM
    return [jnp.where(lo, qf, 0.0).astype(BF16), jnp.where(lo, 0.0, qf).astype(BF16)]


def _attend(q_stack, chunks):
    m = l = acc = None
    for k_c, vt_c, bias_c in chunks:
        s = lax.dot_general(k_c, q_stack, (((1,), (1,)), ((), ())), preferred_element_type=F32)
        if bias_c is not None:
            s = s + bias_c
        mc = jnp.max(s, axis=0, keepdims=True)
        if m is None:
            m_new = mc
            e = jnp.exp(s - m_new)
            l = jnp.sum(e, axis=0, keepdims=True)
            acc = jnp.dot(vt_c, e.astype(BF16), preferred_element_type=F32)
        else:
            m_new = jnp.maximum(m, mc)
            alpha = jnp.exp(m - m_new)
            e = jnp.exp(s - m_new)
            l = alpha * l + jnp.sum(e, axis=0, keepdims=True)
            acc = alpha * acc + jnp.dot(vt_c, e.astype(BF16), preferred_element_type=F32)
        m = m_new
    return acc * (1.0 / l)


def _key_chunks(kc_ref, vc_ref, kl_ref, vl_ref):
    chunks = [(kc_ref[0], vc_ref[0], None)]
    if kl_ref is not None:
        n_lat = kl_ref.shape[1]
        for c0 in range(0, n_lat, KEY_CHUNK):
            chunks.append((kl_ref[0, c0:c0 + KEY_CHUNK, :], vl_ref[0, :, c0:c0 + KEY_CHUNK], None))
    return chunks


def _pick_head_rows(res, tq, p):
    row = lax.broadcasted_iota(jnp.int32, (LANES, tq), 0)
    a = res[:, (2 * p) * tq:(2 * p + 1) * tq]
    b = res[:, (2 * p + 1) * tq:(2 * p + 2) * tq]
    return jnp.where(row < HEAD_DIM, a, b).T


def _ada_kernel(c_ref, w_ref, b_ref, o_ref):
    c = c_ref[...]
    cond = c * (1.0 / (1.0 + jnp.exp(-c)))
    o_ref[0] = jnp.dot(cond, w_ref[0], precision=lax.Precision.HIGHEST,
                       preferred_element_type=F32) + b_ref[0]


def _ada_call(cc, w_ada, b_ada):
    n_layers, d, n_out = w_ada.shape
    rows = cc.shape[0]
    return pl.pallas_call(
        _ada_kernel,
        grid=(n_layers, n_out // ADA_TN),
        in_specs=[
            pl.BlockSpec((rows, d), lambda l, n: (0, 0)),
            pl.BlockSpec((1, d, ADA_TN), lambda l, n: (l, 0, n)),
            pl.BlockSpec((1, 1, ADA_TN), lambda l, n: (l, 0, n)),
        ],
        out_specs=pl.BlockSpec((1, rows, ADA_TN), lambda l, n: (l, 0, n)),
        out_shape=jax.ShapeDtypeStruct((n_layers, rows, n_out), F32),
        compiler_params=_cparams(),
        name="adaln_mod",
    )(cc, w_ada, b_ada.reshape(n_layers, 1, n_out))


def _inproj_kernel(*refs, block_plan, rope):
    if rope:
        x_ref, mod_ref, wqk_ref, wvt_ref, g_ref, cos_ref, sa_ref, sb_ref, oqk_ref, ovt_ref = refs
    else:
        x_ref, mod_ref, wqk_ref, wvt_ref, g_ref, oqk_ref, ovt_ref = refs
    mod = mod_ref[0]
    h = (x_ref[0] * (1.0 + mod[1:2]) + mod[0:1]).astype(BF16)
    qk = jnp.dot(h, wqk_ref[...], preferred_element_type=F32)
    vt = lax.dot_general(wvt_ref[...], h, (((1,), (1,)), ((), ())), preferred_element_type=F32)
    ovt_ref[0] = vt.astype(BF16)
    gm = _group_sum_matrix() if any(p[0] is not None for p in block_plan) else None
    if rope:
        cos, sa, sb = cos_ref[...], sa_ref[...], sb_ref[...]
    for j, (gain_row, do_rope, scale) in enumerate(block_plan):
        blk = qk[:, j * LANES:(j + 1) * LANES]
        if gain_row is not None:
            blk = _head_rms(blk, gm, g_ref[gain_row:gain_row + 1, :])
        if do_rope and rope:
            blk = _rope(blk, cos, sa, sb)
        if scale != 1.0:
            blk = blk * scale
        oqk_ref[0, :, j * LANES:(j + 1) * LANES] = blk.astype(BF16)


def _inproj_call(x, mod, mod_row, wqk, wvt, gains, block_plan, rope_tabs, tm, name):
    b, t, d = x.shape
    wq = wqk.shape[1]
    wv = wvt.shape[0]
    rope = rope_tabs is not None
    mod_map = (lambda bi, i: (bi, 0, 0)) if mod_row is None else (lambda bi, i: (mod_row, 0, 0))
    in_specs = [
        pl.BlockSpec((1, tm, d), lambda bi, i: (bi, i, 0)),
        pl.BlockSpec((1,) + mod.shape[1:], mod_map),
        _const_spec(wqk.shape),
        _const_spec(wvt.shape),
        _const_spec(gains.shape),
    ]
    args = [x, mod, wqk, wvt, gains]
    if rope:
        in_specs += [pl.BlockSpec((tm, LANES), lambda bi, i: (i, 0))] * 3
        args += list(rope_tabs)
    return pl.pallas_call(
        functools.partial(_inproj_kernel, block_plan=block_plan, rope=rope),
        grid=(b, t // tm),
        in_specs=in_specs,
        out_specs=[
            pl.BlockSpec((1, tm, wq), lambda bi, i: (bi, i, 0)),
            pl.BlockSpec((1, wv, tm), lambda bi, i: (bi, 0, i)),
        ],
        out_shape=[jax.ShapeDtypeStruct((b, t, wq), BF16), jax.ShapeDtypeStruct((b, wv, t), BF16)],
        compiler_params=_cparams(),
        name=name,
    )(*args)


def _gqa_kernel(*refs, tq, latent_keys):
    if latent_keys:
        q_ref, kc_ref, vc_ref, kl_ref, vl_ref, o_ref = refs
    else:
        q_ref, kc_ref, vc_ref, o_ref = refs
        kl_ref = vl_ref = None
    q = q_ref[0]
    parts = _split_heads_rows(q[:, :LANES]) + _split_heads_rows(q[:, LANES:])
    res = _attend(jnp.concatenate(parts, axis=0), _key_chunks(kc_ref, vc_ref, kl_ref, vl_ref))
    o_ref[0] = jnp.concatenate([_pick_head_rows(res, tq, 0), _pick_head_rows(res, tq, 1)],
                               axis=1).astype(BF16)


def _gqa_call(qk_q, qk_ctx, vt_ctx, qk_lat, vt_lat, tq, name):
    b, t, _ = qk_q.shape
    n_ctx = qk_ctx.shape[1]
    k_blk0 = (GQA_KV_HEADS * 4 * HEAD_DIM) // LANES
    latent_keys = qk_lat is not None
    in_specs = [
        pl.BlockSpec((1, tq, 2 * LANES), lambda bi, g, i: (bi, i, g)),
        pl.BlockSpec((1, n_ctx, LANES), lambda bi, g, i: (bi, 0, k_blk0 + g)),
        pl.BlockSpec((1, LANES, n_ctx), lambda bi, g, i: (bi, g, 0)),
    ]
    args = [qk_q, qk_ctx, vt_ctx]
    if latent_keys:
        n_lat = qk_lat.shape[1]
        in_specs += [
            pl.BlockSpec((1, n_lat, LANES), lambda bi, g, i: (bi, 0, k_blk0 + g)),
            pl.BlockSpec((1, LANES, n_lat), lambda bi, g, i: (bi, g, 0)),
        ]
        args += [qk_lat, vt_lat]
    return pl.pallas_call(
        functools.partial(_gqa_kernel, tq=tq, latent_keys=latent_keys),
        grid=(b, GQA_KV_HEADS, t // tq),
        in_specs=in_specs,
        out_specs=pl.BlockSpec((1, tq, 2 * LANES), lambda bi, g, i: (bi, i, g)),
        out_shape=jax.ShapeDtypeStruct((b, t, GQA_KV_HEADS * 2 * LANES), BF16),
        compiler_params=_cparams(),
        name=name,
    )(*args)


def _diff_kernel(*refs, tq, latent_keys, lambda_init):
    if latent_keys:
        q_ref, kc_ref, vc_ref, kl_ref, vl_ref, lam_ref, g_ref, o_ref = refs
    else:
        q_ref, kc_ref, vc_ref, lam_ref, g_ref, o_ref = refs
        kl_ref = vl_ref = None
    q_stack = jnp.concatenate(_split_heads_rows(q_ref[0]), axis=0)
    res = _attend(q_stack, _key_chunks(kc_ref, vc_ref, kl_ref, vl_ref))
    lv = lam_ref[...]
    lam = (jnp.exp(jnp.sum(lv[0:1] * lv[1:2], axis=1, keepdims=True))
           - jnp.exp(jnp.sum(lv[2:3] * lv[3:4], axis=1, keepdims=True)) + lambda_init)
    o = (res[:, :tq] - lam * res[:, tq:]).T
    o = o * lax.rsqrt(jnp.mean(o * o, axis=-1, keepdims=True) + EPS) * g_ref[...]
    o_ref[0] = (o * (1.0 - lambda_init)).astype(BF16)


def _diff_call(qk_q, qk_ctx, vt_ctx, qk_lat, vt_lat, lam_vec, subln_g, lambda_init, tq, name):
    b, t, wqk = qk_q.shape
    n_ctx = qk_ctx.shape[1]
    n_heads = wqk // (4 * LANES)
    q_blk0, k_blk0, v_blk0 = 2 * n_heads, 3 * n_heads, n_heads
    latent_keys = qk_lat is not None
    in_specs = [
        pl.BlockSpec((1, tq, LANES), lambda bi, h, i: (bi, i, q_blk0 + h)),
        pl.BlockSpec((1, n_ctx, LANES), lambda bi, h, i: (bi, 0, k_blk0 + h)),
        pl.BlockSpec((1, LANES, n_ctx), lambda bi, h, i: (bi, v_blk0 + h, 0)),
    ]
    args = [qk_q, qk_ctx, vt_ctx]
    if latent_keys:
        n_lat = qk_lat.shape[1]
        in_specs += [
            pl.BlockSpec((1, n_lat, LANES), lambda bi, h, i: (bi, 0, k_blk0 + h)),
            pl.BlockSpec((1, LANES, n_lat), lambda bi, h, i: (bi, v_blk0 + h, 0)),
        ]
        args += [qk_lat, vt_lat]
    in_specs += [_const_spec(lam_vec.shape), _const_spec((1, LANES))]
    args += [lam_vec, subln_g.reshape(1, LANES)]
    return pl.pallas_call(
        functools.partial(_diff_kernel, tq=tq, latent_keys=latent_keys, lambda_init=lambda_init),
        grid=(b, n_heads, t // tq),
        in_specs=in_specs,
        out_specs=pl.BlockSpec((1, tq, LANES), lambda bi, h, i: (bi, i, h)),
        out_shape=jax.ShapeDtypeStruct((b, t, n_heads * LANES), BF16),
        compiler_params=_cparams(),
        name=name,
    )(*args)


def _na_kernel(q_ref, kc_ref, vc_ref, k0_ref, k1_ref, k2_ref, v0_ref, v1_ref, v2_ref, bias_ref, o_ref):
    tq = q_ref.shape[1]
    q_stack = jnp.concatenate(_split_heads_rows(q_ref[0]), axis=0)
    chunks = [(kc_ref[0], vc_ref[0], None)]
    for c, (k_ref, v_ref) in enumerate(((k0_ref, v0_ref), (k1_ref, v1_ref), (k2_ref, v2_ref))):
        chunks.append((k_ref[0], v_ref[0], bias_ref[0, 0, c * tq:(c + 1) * tq, :]))
    o_ref[0] = _pick_head_rows(_attend(q_stack, chunks), tq, 0).astype(BF16)


def _na_ctx_kernel(q_ref, kc_ref, vc_ref, o_ref):
    tq = q_ref.shape[1]
    q_stack = jnp.concatenate(_split_heads_rows(q_ref[0]), axis=0)
    o_ref[0] = _pick_head_rows(_attend(q_stack, [(kc_ref[0], vc_ref[0], None)]), tq, 0).astype(BF16)


def _na_call(qk_lat, vt_lat, qk_ctx, vt_ctx, bias_tab, name):
    b, s, _ = qk_lat.shape
    n_ctx = qk_ctx.shape[1]
    tq = NA_ROWS_PER_BLOCK * GRID_W
    nj = s // tq
    n_pairs = bias_tab.shape[1]
    k_blk0 = n_pairs

    def tb(j):
        return jnp.clip(j - 1, 0, nj - 3)

    def cls(j):
        return jnp.where(j == 0, 0, jnp.where(j == nj - 1, 2, 1))

    k_specs = [pl.BlockSpec((1, tq, LANES), functools.partial(lambda p, j, bi, d: (bi, tb(j) + d, k_blk0 + p), d=d))
               for d in range(3)]
    v_specs = [pl.BlockSpec((1, LANES, tq), functools.partial(lambda p, j, bi, d: (bi, p, tb(j) + d), d=d))
               for d in range(3)]
    return pl.pallas_call(
        _na_kernel,
        grid=(n_pairs, nj, b),
        in_specs=[
            pl.BlockSpec((1, tq, LANES), lambda p, j, bi: (bi, j, p)),
            pl.BlockSpec((1, n_ctx, LANES), lambda p, j, bi: (bi, 0, k_blk0 + p)),
            pl.BlockSpec((1, LANES, n_ctx), lambda p, j, bi: (bi, p, 0)),
            *k_specs, *v_specs,
            pl.BlockSpec((1, 1, 3 * tq, 2 * tq), lambda p, j, bi: (cls(j), p, 0, 0)),
        ],
        out_specs=pl.BlockSpec((1, tq, LANES), lambda p, j, bi: (bi, j, p)),
        out_shape=jax.ShapeDtypeStruct((b, s, n_pairs * LANES), BF16),
        compiler_params=_cparams(),
        name=name,
    )(qk_lat, qk_ctx, vt_ctx, qk_lat, qk_lat, qk_lat, vt_lat, vt_lat, vt_lat, bias_tab)


def _na_ctx_call(qk_ctx, vt_ctx, n_pairs, name):
    b, n_ctx, _ = qk_ctx.shape
    k_blk0 = n_pairs
    return pl.pallas_call(
        _na_ctx_kernel,
        grid=(b, n_pairs),
        in_specs=[
            pl.BlockSpec((1, n_ctx, LANES), lambda bi, p: (bi, 0, p)),
            pl.BlockSpec((1, n_ctx, LANES), lambda bi, p: (bi, 0, k_blk0 + p)),
            pl.BlockSpec((1, LANES, n_ctx), lambda bi, p: (bi, p, 0)),
        ],
        out_specs=pl.BlockSpec((1, n_ctx, LANES), lambda bi, p: (bi, 0, p)),
        out_shape=jax.ShapeDtypeStruct((b, n_ctx, n_pairs * LANES), BF16),
        compiler_params=_cparams(),
        name=name,
    )(qk_ctx, qk_ctx, vt_ctx)


def _na_bias_tables(rpb, seq):
    n_heads = rpb.shape[0]
    rows = seq // GRID_W
    tq = NA_ROWS_PER_BLOCK * GRID_W
    nj = seq // tq
    kh = min(NA_KH, rows)
    tabs = []
    for j in (0, 1, nj - 1):
        ws = NA_ROWS_PER_BLOCK * int(np.clip(j - 1, 0, nj - 3))
        kk = np.arange(3 * tq)
        kr, kc = ws + kk // GRID_W, kk % GRID_W
        qq = np.arange(tq)
        r, cq = NA_ROWS_PER_BLOCK * j + qq // GRID_W, qq % GRID_W
        rs = np.clip(r - kh // 2, 0, rows - kh)
        cs = np.clip(cq - NA_KW // 2, 0, GRID_W - NA_KW)
        ok = ((kr[:, None] >= rs[None, :]) & (kr[:, None] < rs[None, :] + kh)
              & (kc[:, None] >= cs[None, :]) & (kc[:, None] < cs[None, :] + NA_KW))
        dr = np.clip(kr[:, None] - r[None, :] + NA_KH - 1, 0, 2 * NA_KH - 2)
        dc = np.clip(kc[:, None] - cq[None, :] + NA_KW - 1, 0, 2 * NA_KW - 2)
        tabs.append(jnp.where(ok[None], rpb[:, dr, dc].astype(F32), -jnp.inf))
    tab = jnp.stack(tabs)
    tab = tab.reshape(3, n_heads // 2, 2, 3 * tq, tq).transpose(0, 1, 3, 2, 4)
    return tab.reshape(3, n_heads // 2, 3 * tq, 2 * tq)


def _outproj_kernel(*refs, n_att):
    att_refs = refs[:n_att]
    x_ref, mod_ref, w_ref, lng_ref, lnb_ref, o_ref = refs[n_att:]
    y = None
    k0 = 0
    for a_ref in att_refs:
        kw = a_ref.shape[2]
        part = jnp.dot(a_ref[0], w_ref[k0:k0 + kw, :], preferred_element_type=F32)
        y = part if y is None else y + part
        k0 += kw
    z = ALPHA * x_ref[0] + mod_ref[0][2:3] * y
    o_ref[0] = _layer_norm(z, lng_ref[...], lnb_ref[...])


def _outproj_call(atts, x, mod, mod_row, w_o, ln_g, ln_b, tm, name):
    b, t, d = x.shape
    mod_map = (lambda bi, i: (bi, 0, 0)) if mod_row is None else (lambda bi, i: (mod_row, 0, 0))
    in_specs = [pl.BlockSpec((1, tm, a.shape[2]), lambda bi, i: (bi, i, 0)) for a in atts]
    in_specs += [
        pl.BlockSpec((1, tm, d), lambda bi, i: (bi, i, 0)),
        pl.BlockSpec((1,) + mod.shape[1:], mod_map),
        _const_spec(w_o.shape),
        _const_spec((1, d)),
        _const_spec((1, d)),
    ]
    return pl.pallas_call(
        functools.partial(_outproj_kernel, n_att=len(atts)),
        grid=(b, t // tm),
        in_specs=in_specs,
        out_specs=pl.BlockSpec((1, tm, d), lambda bi, i: (bi, i, 0)),
        out_shape=jax.ShapeDtypeStruct((b, t, d), F32),
        compiler_params=_cparams(),
        name=name,
    )(*atts, x, mod, w_o, ln_g.reshape(1, d), ln_b.reshape(1, d))


def _ffn_kernel(*refs, tm, halo):
    if halo:
        x_ref, xp_ref, xn_ref, mod_ref, wup_ref, cw_ref, cb_ref, wdn_ref, lng_ref, lnb_ref, o_ref, hs_ref, up_ref = refs
    else:
        x_ref, mod_ref, wup_ref, cw_ref, cb_ref, wdn_ref, lng_ref, lnb_ref, o_ref, hs_ref, up_ref = refs
    d = x_ref.shape[2]
    mod = mod_ref[0]
    shift, scale, gate = mod[3:4], mod[4:5], mod[5:6]
    x = x_ref[0]
    hs_ref[HALO:HALO + tm, :] = (x * (1.0 + scale) + shift).astype(BF16)
    zeros8 = jnp.zeros((SUBLANES_F32, d), F32)
    if halo:
        i = pl.program_id(1)
        last = pl.num_programs(1) - 1
        hp = jnp.where(i > 0, xp_ref[0] * (1.0 + scale) + shift, 0.0)
        hn = jnp.where(i < last, xn_ref[0] * (1.0 + scale) + shift, 0.0)
    else:
        hp = hn = zeros8
    hs_ref[0:HALO, :] = jnp.concatenate([zeros8, hp], axis=0).astype(BF16)
    hs_ref[HALO + tm:HALO + tm + HALO, :] = jnp.concatenate([hn, zeros8], axis=0).astype(BF16)
    hs = hs_ref[...]
    acc = None
    n_chunks = wdn_ref.shape[0] // FF_CHUNK
    for c in range(n_chunks):
        cols = slice(c * 2 * FF_CHUNK, (c + 1) * 2 * FF_CHUNK)
        up_ref[...] = jnp.dot(hs, wup_ref[:, cols], preferred_element_type=F32)
        w = cw_ref[:, cols]
        u = (up_ref[HALO - 1:HALO - 1 + tm, :] * w[0:1]
             + up_ref[HALO:HALO + tm, :] * w[1:2]
             + up_ref[HALO + 1:HALO + 1 + tm, :] * w[2:3]
             + cb_ref[:, cols])
        act = (_gelu_tanh(u[:, FF_CHUNK:]) * u[:, :FF_CHUNK]).astype(BF16)
        part = jnp.dot(act, wdn_ref[c * FF_CHUNK:(c + 1) * FF_CHUNK, :], preferred_element_type=F32)
        acc = part if acc is None else acc + part
    z = ALPHA * x + gate * acc
    o_ref[0] = _layer_norm(z, lng_ref[...], lnb_ref[...])


def _ffn_call(x, mod, mod_row, w_up, conv_w, conv_b, w_down, ln_g, ln_b, tm, name):
    b, t, d = x.shape
    halo = t > tm
    mod_map = (lambda bi, i: (bi, 0, 0)) if mod_row is None else (lambda bi, i: (mod_row, 0, 0))
    in_specs = [pl.BlockSpec((1, tm, d), lambda bi, i: (bi, i, 0))]
    args = [x]
    if halo:
        r8 = tm // SUBLANES_F32
        n8 = t // SUBLANES_F32
        in_specs += [
            pl.BlockSpec((1, SUBLANES_F32, d), lambda bi, i: (bi, jnp.maximum(i * r8 - 1, 0), 0)),
            pl.BlockSpec((1, SUBLANES_F32, d), lambda bi, i: (bi, jnp.minimum((i + 1) * r8, n8 - 1), 0)),
        ]
        args += [x, x]
    in_specs += [
        pl.BlockSpec((1,) + mod.shape[1:], mod_map),
        _const_spec(w_up.shape),
        _const_spec(conv_w.shape),
        _const_spec(conv_b.shape),
        _const_spec(w_down.shape),
        _const_spec((1, d)),
        _const_spec((1, d)),
    ]
    args += [mod, w_up, conv_w, conv_b, w_down, ln_g.reshape(1, d), ln_b.reshape(1, d)]
    return pl.pallas_call(
        functools.partial(_ffn_kernel, tm=tm, halo=halo),
        grid=(b, t // tm),
        in_specs=in_specs,
        out_specs=pl.BlockSpec((1, tm, d), lambda bi, i: (bi, i, 0)),
        out_shape=jax.ShapeDtypeStruct((b, t, d), F32),
        scratch_shapes=[
            pltpu.VMEM((tm + 2 * HALO, d), BF16),
            pltpu.VMEM((tm + 2 * HALO, 2 * FF_CHUNK), F32),
        ],
        compiler_params=_cparams(),
        name=name,
    )(*args)


def _rope_tables(seq):
    n_freq = HEAD_DIM // 4
    inv = ROPE_BASE ** (-jnp.arange(n_freq, dtype=F32) / n_freq)
    t = jnp.arange(seq)
    row = (t // GRID_W).astype(F32)
    col = (t % GRID_W).astype(F32)
    ar = row[:, None] * inv
    ac = col[:, None] * inv
    ang = jnp.concatenate([ar, ar, ac, ac], -1)
    cos, sin = jnp.cos(ang), jnp.sin(ang)
    even_chunk = ((np.arange(HEAD_DIM) // n_freq) % 2 == 0)[None, :]
    sin_a = jnp.where(even_chunk, -sin, 0.0)
    sin_b = jnp.where(even_chunk, 0.0, sin)
    return tuple(jnp.concatenate([a, a], axis=1) for a in (cos, sin_a, sin_b))


def _interleave_ff(a, d_ff):
    lead = a.shape[:-1]
    a = a.reshape(lead + (2, d_ff // FF_CHUNK, FF_CHUNK))
    a = jnp.swapaxes(a, -3, -2)
    return a.reshape(lead + (2 * d_ff,))


def _dup_heads(w, n_heads):
    d = w.shape[0]
    w = w.reshape(d, n_heads, HEAD_DIM)
    return jnp.concatenate([w, w], axis=-1).reshape(d, n_heads * PAIR)


def kernel(x, c, ctx, c_ctx, w_ada, b_ada, ln_g, ln_b, w_in_ab, w_o_ab, na_rpb, diff_lambda, diff_subln,
           w_in_c, w_o_c, gqa_qk_norm, w_up, conv_w, conv_b, w_down):
    b, s, d = x.shape
    n_ctx = ctx.shape[1]
    d_ff = w_down.shape[1]
    na_w = na_rpb.shape[1] * HEAD_DIM
    diff_w = d - na_w
    gqa_kw = GQA_KV_HEADS * HEAD_DIM
    tm_lat = min(TM_LATENT, s)

    mod_rows = -(-(b + 1) // SUBLANES_F32) * SUBLANES_F32
    ctx_row = b
    cc = jnp.concatenate([c, c_ctx[None, :], jnp.zeros((mod_rows - b - 1, d), F32)], axis=0)
    mod_all = _ada_call(cc, w_ada, b_ada)
    rope_tabs = _rope_tables(s)
    ones_gain = jnp.ones((2, LANES), F32)

    xc = ctx
    for l in range(DEPTH):
        need_ctx = l < DEPTH - 1
        i = l // 2
        mod = mod_all[l].reshape(mod_rows, 6, d)
        if l % 2 == 0:
            lambda_init = 0.8 - 0.6 * math.exp(-0.3 * l)
            w = w_in_ab[i]
            o0 = 3 * na_w
            wqk = jnp.concatenate([w[:, :2 * na_w], w[:, o0:o0 + 2 * diff_w]], axis=1).astype(BF16)
            wvt = jnp.concatenate([w[:, 2 * na_w:o0], w[:, o0 + 2 * diff_w:]], axis=1).T.astype(BF16)
            nb = na_w // LANES
            plan = ([(None, False, QK_SCALE)] * nb + [(None, False, 1.0)] * nb
                    + [(None, True, QK_SCALE)] * (diff_w // LANES) + [(None, True, 1.0)] * (diff_w // LANES))
            gains = ones_gain
        else:
            w = w_in_c[i]
            q_w = d
            wqk = jnp.concatenate([w[:, :q_w], _dup_heads(w[:, q_w:q_w + gqa_kw], GQA_KV_HEADS)],
                                  axis=1).astype(BF16)
            wvt = _dup_heads(w[:, q_w + gqa_kw:], GQA_KV_HEADS).T.astype(BF16)
            plan = [(0, True, QK_SCALE)] * (q_w // LANES) + [(1, True, 1.0)] * GQA_KV_HEADS
            gains = jnp.concatenate([gqa_qk_norm[i], gqa_qk_norm[i]], axis=1)
        plan = tuple(plan)

        qk_lat, vt_lat = _inproj_call(x, mod, None, wqk, wvt, gains, plan, rope_tabs, tm_lat, f"inproj_lat_{l}")
        qk_ctx, vt_ctx = _inproj_call(xc, mod, ctx_row, wqk, wvt, gains, plan, None, n_ctx, f"inproj_ctx_{l}")

        if l % 2 == 0:
            bias_tab = _na_bias_tables(na_rpb[i], s)
            atts = [
                _na_call(qk_lat, vt_lat, qk_ctx, vt_ctx, bias_tab, f"na_lat_{l}"),
                _diff_call(qk_lat, qk_ctx, vt_ctx, qk_lat, vt_lat, diff_lambda[i], diff_subln[i], lambda_init,
                           min(TQ_DIFF, s), f"diff_lat_{l}"),
            ]
            w_o = w_o_ab[i].astype(BF16)
            if need_ctx:
                atts_c = [
                    _na_ctx_call(qk_ctx, vt_ctx, na_w // LANES, f"na_ctx_{l}"),
                    _diff_call(qk_ctx, qk_ctx, vt_ctx, None, None, diff_lambda[i], diff_subln[i], lambda_init,
                               n_ctx, f"diff_ctx_{l}"),
                ]
        else:
            atts = [_gqa_call(qk_lat, qk_ctx, vt_ctx, qk_lat, vt_lat, min(TQ_GQA, s), f"gqa_lat_{l}")]
            w_o = w_o_c[i].astype(BF16)
            if need_ctx:
                atts_c = [_gqa_call(qk_ctx, qk_ctx, vt_ctx, None, None, n_ctx, f"gqa_ctx_{l}")]

        w_up_l = _interleave_ff(w_up[l], d_ff).astype(BF16)
        conv_w_l = _interleave_ff(conv_w[l], d_ff)
        conv_b_l = _interleave_ff(conv_b[l], d_ff).reshape(1, 2 * d_ff)
        w_down_l = w_down[l].astype(BF16)

        x = _outproj_call(atts, x, mod, None, w_o, ln_g[l, 0], ln_b[l, 0], tm_lat, f"outproj_lat_{l}")
        x = _ffn_call(x, mod, None, w_up_l, conv_w_l, conv_b_l, w_down_l, ln_g[l, 1], ln_b[l, 1],
                      tm_lat, f"ffn_lat_{l}")
        if need_ctx:
            xc = _outproj_call(atts_c, xc, mod, ctx_row, w_o, ln_g[l, 0], ln_b[l, 0], n_ctx, f"outproj_ctx_{l}")
            xc = _ffn_call(xc, mod, ctx_row, w_up_l, conv_w_l, conv_b_l, w_down_l, ln_g[l, 1], ln_b[l, 1],
                           n_ctx, f"ffn_ctx_{l}")
    return x
```

```python
import functools
import math

import numpy as np
import jax
import jax.numpy as jnp
from jax import lax
from jax.experimental import pallas as pl
from jax.experimental.pallas import tpu as pltpu

F32 = jnp.float32
BF16 = jnp.bfloat16

DEPTH = 4
GRID_W = 64
HEAD_DIM = 64
NA_KH = 8
NA_KW = 16
GQA_KV_HEADS = 4
CONV_WIDTH = 3
ROPE_BASE = 10000.0
EPS = 1e-6
ALPHA = (2 * DEPTH) ** 0.25
QK_SCALE = HEAD_DIM ** -0.5

LANES = 128
SUBLANES_F32 = 8
SUBLANES_BF16 = 16
MXU_DIM = 256
VMEM_LIMIT_BYTES = 56 * 1024 * 1024

PAIR = 2 * HEAD_DIM
assert PAIR == LANES

TM_LATENT = 512
TQ_GQA = 256
TQ_DIFF = 512
NA_ROWS_PER_BLOCK = 4
KEY_CHUNK = 512
FF_CHUNK = MXU_DIM
ADA_TN = 1536
HALO = SUBLANES_BF16


def _cparams():
    return pltpu.CompilerParams(vmem_limit_bytes=VMEM_LIMIT_BYTES)


def _const_spec(shape):
    nd = len(shape)
    return pl.BlockSpec(shape, lambda *_: (0,) * nd, pipeline_mode=pl.Buffered(1))


def _layer_norm(z, g, b):
    mu = jnp.mean(z, axis=-1, keepdims=True)
    d = z - mu
    var = jnp.mean(d * d, axis=-1, keepdims=True)
    return d * lax.rsqrt(var + EPS) * g + b


def _gelu_tanh(x):
    return 0.5 * x * (1.0 + jnp.tanh(math.sqrt(2.0 / math.pi) * (x + 0.044715 * (x * x * x))))


def _group_sum_matrix():
    r = lax.broadcasted_iota(jnp.int32, (2 * LANES, LANES), 0)
    c = lax.broadcasted_iota(jnp.int32, (2 * LANES, LANES), 1)
    same = ((r % LANES) // HEAD_DIM) == (c // HEAD_DIM)
    return jnp.where(same, 1.0, 0.0).astype(BF16)


def _head_rms(xb, gm, g):
    x2 = xb * xb
    hi = x2.astype(BF16)
    lo = (x2 - hi.astype(F32)).astype(BF16)
    ssq = jnp.dot(jnp.concatenate([hi, lo], axis=1), gm, preferred_element_type=F32)
    return xb * lax.rsqrt(ssq * (1.0 / HEAD_DIM) + EPS) * g


def _rope(xb, cos, sin_a, sin_b):
    return (xb * cos
            + pltpu.roll(xb, LANES - HEAD_DIM // 4, 1) * sin_a
            + pltpu.roll(xb, HEAD_DIM // 4, 1) * sin_b)


def _split_heads_rows(q_pair):
    qf = q_pair.astype(F32)
    lane = lax.broadcasted_iota(jnp.int32, qf.shape, 1)
    lo = lane < HEAD_DIM
    return [jnp.where(lo, qf, 0.0).astype(BF16), jnp.where(lo, 0.0, qf).astype(BF16)]


def _attend(q_stack, chunks):
    m = l = acc = None
    for k_c, vt_c, bias_c in chunks:
        s = lax.dot_general(k_c, q_stack, (((1,), (1,)), ((), ())), preferred_element_type=F32)
        if bias_c is not None:
            s = s + bias_c
        mc = jnp.max(s, axis=0, keepdims=True)
        if m is None:
            m_new = mc
            e = jnp.exp(s - m_new)
            l = jnp.sum(e, axis=0, keepdims=True)
            acc = jnp.dot(vt_c, e.astype(BF16), preferred_element_type=F32)
        else:
            m_new = jnp.maximum(m, mc)
            alpha = jnp.exp(m - m_new)
            e = jnp.exp(s - m_new)
            l = alpha * l + jnp.sum(e, axis=0, keepdims=True)
            acc = alpha * acc + jnp.dot(vt_c, e.astype(BF16), preferred_element_type=F32)
        m = m_new
    return acc * (1.0 / l)


def _key_chunks(kc_ref, vc_ref, kl_ref, vl_ref):
    chunks = [(kc_ref[0], vc_ref[0], None)]
    if kl_ref is not None:
        n_lat = kl_ref.shape[1]
        for c0 in range(0, n_lat, KEY_CHUNK):
            chunks.append((kl_ref[0, c0:c0 + KEY_CHUNK, :], vl_ref[0, :, c0:c0 + KEY_CHUNK], None))
    return chunks


def _pick_head_rows(res, tq, p):
    row = lax.broadcasted_iota(jnp.int32, (LANES, tq), 0)
    a = res[:, (2 * p) * tq:(2 * p + 1) * tq]
    b = res[:, (2 * p + 1) * tq:(2 * p + 2) * tq]
    return jnp.where(row < HEAD_DIM, a, b).T


def _ada_kernel(c_ref, w_ref, b_ref, o_ref):
    c = c_ref[...]
    cond = c * (1.0 / (1.0 + jnp.exp(-c)))
    o_ref[0] = jnp.dot(cond, w_ref[0], precision=lax.Precision.HIGHEST,
                       preferred_element_type=F32) + b_ref[0]


def _ada_call(cc, w_ada, b_ada):
    n_layers, d, n_out = w_ada.shape
    rows = cc.shape[0]
    return pl.pallas_call(
        _ada_kernel,
        grid=(n_layers, n_out // ADA_TN),
        in_specs=[
            pl.BlockSpec((rows, d), lambda l, n: (0, 0)),
            pl.BlockSpec((1, d, ADA_TN), lambda l, n: (l, 0, n)),
            pl.BlockSpec((1, 1, ADA_TN), lambda l, n: (l, 0, n)),
        ],
        out_specs=pl.BlockSpec((1, rows, ADA_TN), lambda l, n: (l, 0, n)),
        out_shape=jax.ShapeDtypeStruct((n_layers, rows, n_out), F32),
        compiler_params=_cparams(),
        name="adaln_mod",
    )(cc, w_ada, b_ada.reshape(n_layers, 1, n_out))


def _inproj_kernel(*refs, block_plan, rope):
    if rope:
        x_ref, mod_ref, wqk_ref, wvt_ref, g_ref, cos_ref, sa_ref, sb_ref, oqk_ref, ovt_ref = refs
    else:
        x_ref, mod_ref, wqk_ref, wvt_ref, g_ref, oqk_ref, ovt_ref = refs
    mod = mod_ref[0]
    h = (x_ref[0] * (1.0 + mod[1:2]) + mod[0:1]).astype(BF16)
    qk = jnp.dot(h, wqk_ref[...], preferred_element_type=F32)
    vt = lax.dot_general(wvt_ref[...], h, (((1,), (1,)), ((), ())), preferred_element_type=F32)
    ovt_ref[0] = vt.astype(BF16)
    gm = _group_sum_matrix() if any(p[0] is not None for p in block_plan) else None
    if rope:
        cos, sa, sb = cos_ref[...], sa_ref[...], sb_ref[...]
    for j, (gain_row, do_rope, scale) in enumerate(block_plan):
        blk = qk[:, j * LANES:(j + 1) * LANES]
        if gain_row is not None:
            blk = _head_rms(blk, gm, g_ref[gain_row:gain_row + 1, :])
        if do_rope and rope:
            blk = _rope(blk, cos, sa, sb)
        if scale != 1.0:
            blk = blk * scale
        oqk_ref[0, :, j * LANES:(j + 1) * LANES] = blk.astype(BF16)


def _inproj_call(x, mod, mod_row, wqk, wvt, gains, block_plan, rope_tabs, tm, name):
    b, t, d = x.shape
    wq = wqk.shape[1]
    wv = wvt.shape[0]
    rope = rope_tabs is not None
    mod_map = (lambda bi, i: (bi, 0, 0)) if mod_row is None else (lambda bi, i: (mod_row, 0, 0))
    in_specs = [
        pl.BlockSpec((1, tm, d), lambda bi, i: (bi, i, 0)),
        pl.BlockSpec((1,) + mod.shape[1:], mod_map),
        _const_spec(wqk.shape),
        _const_spec(wvt.shape),
        _const_spec(gains.shape),
    ]
    args = [x, mod, wqk, wvt, gains]
    if rope:
        in_specs += [pl.BlockSpec((tm, LANES), lambda bi, i: (i, 0))] * 3
        args += list(rope_tabs)
    return pl.pallas_call(
        functools.partial(_inproj_kernel, block_plan=block_plan, rope=rope),
        grid=(b, t // tm),
        in_specs=in_specs,
        out_specs=[
            pl.BlockSpec((1, tm, wq), lambda bi, i: (bi, i, 0)),
            pl.BlockSpec((1, wv, tm), lambda bi, i: (bi, 0, i)),
        ],
        out_shape=[jax.ShapeDtypeStruct((b, t, wq), BF16), jax.ShapeDtypeStruct((b, wv, t), BF16)],
        compiler_params=_cparams(),
        name=name,
    )(*args)


def _gqa_kernel(*refs, tq, latent_keys):
    if latent_keys:
        q_ref, kc_ref, vc_ref, kl_ref, vl_ref, o_ref = refs
    else:
        q_ref, kc_ref, vc_ref, o_ref = refs
        kl_ref = vl_ref = None
    q = q_ref[0]
    parts = _split_heads_rows(q[:, :LANES]) + _split_heads_rows(q[:, LANES:])
    res = _attend(jnp.concatenate(parts, axis=0), _key_chunks(kc_ref, vc_ref, kl_ref, vl_ref))
    o_ref[0] = jnp.concatenate([_pick_head_rows(res, tq, 0), _pick_head_rows(res, tq, 1)],
                               axis=1).astype(BF16)


def _gqa_call(qk_q, qk_ctx, vt_ctx, qk_lat, vt_lat, tq, name):
    b, t, _ = qk_q.shape
    n_ctx = qk_ctx.shape[1]
    k_blk0 = (GQA_KV_HEADS * 4 * HEAD_DIM) // LANES
    latent_keys = qk_lat is not None
    in_specs = [
        pl.BlockSpec((1, tq, 2 * LANES), lambda bi, g, i: (bi, i, g)),
        pl.BlockSpec((1, n_ctx, LANES), lambda bi, g, i: (bi, 0, k_blk0 + g)),
        pl.BlockSpec((1, LANES, n_ctx), lambda bi, g, i: (bi, g, 0)),
    ]
    args = [qk_q, qk_ctx, vt_ctx]
    if latent_keys:
        n_lat = qk_lat.shape[1]
        in_specs += [
            pl.BlockSpec((1, n_lat, LANES), lambda bi, g, i: (bi, 0, k_blk0 + g)),
            pl.BlockSpec((1, LANES, n_lat), lambda bi, g, i: (bi, g, 0)),
        ]
        args += [qk_lat, vt_lat]
    return pl.pallas_call(
        functools.partial(_gqa_kernel, tq=tq, latent_keys=latent_keys),
        grid=(b, GQA_KV_HEADS, t // tq),
        in_specs=in_specs,
        out_specs=pl.BlockSpec((1, tq, 2 * LANES), lambda bi, g, i: (bi, i, g)),
        out_shape=jax.ShapeDtypeStruct((b, t, GQA_KV_HEADS * 2 * LANES), BF16),
        compiler_params=_cparams(),
        name=name,
    )(*args)


def _diff_kernel(*refs, tq, latent_keys, lambda_init):
    if latent_keys:
        q_ref, kc_ref, vc_ref, kl_ref, vl_ref, lam_ref, g_ref, o_ref = refs
    else:
        q_ref, kc_ref, vc_ref, lam_ref, g_ref, o_ref = refs
        kl_ref = vl_ref = None
    q_stack = jnp.concatenate(_split_heads_rows(q_ref[0]), axis=0)
    res = _attend(q_stack, _key_chunks(kc_ref, vc_ref, kl_ref, vl_ref))
    lv = lam_ref[...]
    lam = (jnp.exp(jnp.sum(lv[0:1] * lv[1:2], axis=1, keepdims=True))
           - jnp.exp(jnp.sum(lv[2:3] * lv[3:4], axis=1, keepdims=True)) + lambda_init)
    o = (res[:, :tq] - lam * res[:, tq:]).T
    o = o * lax.rsqrt(jnp.mean(o * o, axis=-1, keepdims=True) + EPS) * g_ref[...]
    o_ref[0] = (o * (1.0 - lambda_init)).astype(BF16)


def _diff_call(qk_q, qk_ctx, vt_ctx, qk_lat, vt_lat, lam_vec, subln_g, lambda_init, tq, name):
    b, t, wqk = qk_q.shape
    n_ctx = qk_ctx.shape[1]
    n_heads = wqk // (4 * LANES)
    q_blk0, k_blk0, v_blk0 = 2 * n_heads, 3 * n_heads, n_heads
    latent_keys = qk_lat is not None
    in_specs = [
        pl.BlockSpec((1, tq, LANES), lambda bi, h, i: (bi, i, q_blk0 + h)),
        pl.BlockSpec((1, n_ctx, LANES), lambda bi, h, i: (bi, 0, k_blk0 + h)),
        pl.BlockSpec((1, LANES, n_ctx), lambda bi, h, i: (bi, v_blk0 + h, 0)),
    ]
    args = [qk_q, qk_ctx, vt_ctx]
    if latent_keys:
        n_lat = qk_lat.shape[1]
        in_specs += [
            pl.BlockSpec((1, n_lat, LANES), lambda bi, h, i: (bi, 0, k_blk0 + h)),
            pl.BlockSpec((1, LANES, n_lat), lambda bi, h, i: (bi, v_blk0 + h, 0)),
        ]
        args += [qk_lat, vt_lat]
    in_specs += [_const_spec(lam_vec.shape), _const_spec((1, LANES))]
    args += [lam_vec, subln_g.reshape(1, LANES)]
    return pl.pallas_call(
        functools.partial(_diff_kernel, tq=tq, latent_keys=latent_keys, lambda_init=lambda_init),
        grid=(b, n_heads, t // tq),
        in_specs=in_specs,
        out_specs=pl.BlockSpec((1, tq, LANES), lambda bi, h, i: (bi, i, h)),
        out_shape=jax.ShapeDtypeStruct((b, t, n_heads * LANES), BF16),
        compiler_params=_cparams(),
        name=name,
    )(*args)


def _na_kernel(q_ref, kc_ref, vc_ref, k0_ref, k1_ref, k2_ref, v0_ref, v1_ref, v2_ref, bias_ref, o_ref):
    tq = q_ref.shape[1]
    q_stack = jnp.concatenate(_split_heads_rows(q_ref[0]), axis=0)
    chunks = [(kc_ref[0], vc_ref[0], None)]
    for c, (k_ref, v_ref) in enumerate(((k0_ref, v0_ref), (k1_ref, v1_ref), (k2_ref, v2_ref))):
        chunks.append((k_ref[0], v_ref[0], bias_ref[0, 0, c * tq:(c + 1) * tq, :]))
    o_ref[0] = _pick_head_rows(_attend(q_stack, chunks), tq, 0).astype(BF16)


def _na_ctx_kernel(q_ref, kc_ref, vc_ref, o_ref):
    tq = q_ref.shape[1]
    q_stack = jnp.concatenate(_split_heads_rows(q_ref[0]), axis=0)
    o_ref[0] = _pick_head_rows(_attend(q_stack, [(kc_ref[0], vc_ref[0], None)]), tq, 0).astype(BF16)


def _na_call(qk_lat, vt_lat, qk_ctx, vt_ctx, bias_tab, name):
    b, s, _ = qk_lat.shape
    n_ctx = qk_ctx.shape[1]
    tq = NA_ROWS_PER_BLOCK * GRID_W
    nj = s // tq
    n_pairs = bias_tab.shape[1]
    k_blk0 = n_pairs

    def tb(j):
        return jnp.clip(j - 1, 0, nj - 3)

    def cls(j):
        return jnp.where(j == 0, 0, jnp.where(j == nj - 1, 2, 1))

    k_specs = [pl.BlockSpec((1, tq, LANES), functools.partial(lambda p, j, bi, d: (bi, tb(j) + d, k_blk0 + p), d=d))
               for d in range(3)]
    v_specs = [pl.BlockSpec((1, LANES, tq), functools.partial(lambda p, j, bi, d: (bi, p, tb(j) + d), d=d))
               for d in range(3)]
    return pl.pallas_call(
        _na_kernel,
        grid=(n_pairs, nj, b),
        in_specs=[
            pl.BlockSpec((1, tq, LANES), lambda p, j, bi: (bi, j, p)),
            pl.BlockSpec((1, n_ctx, LANES), lambda p, j, bi: (bi, 0, k_blk0 + p)),
            pl.BlockSpec((1, LANES, n_ctx), lambda p, j, bi: (bi, p, 0)),
            *k_specs, *v_specs,
            pl.BlockSpec((1, 1, 3 * tq, 2 * tq), lambda p, j, bi: (cls(j), p, 0, 0)),
        ],
        out_specs=pl.BlockSpec((1, tq, LANES), lambda p, j, bi: (bi, j, p)),
        out_shape=jax.ShapeDtypeStruct((b, s, n_pairs * LANES), BF16),
        compiler_params=_cparams(),
        name=name,
    )(qk_lat, qk_ctx, vt_ctx, qk_lat, qk_lat, qk_lat, vt_lat, vt_lat, vt_lat, bias_tab)


def _na_ctx_call(qk_ctx, vt_ctx, n_pairs, name):
    b, n_ctx, _ = qk_ctx.shape
    k_blk0 = n_pairs
    return pl.pallas_call(
        _na_ctx_kernel,
        grid=(b, n_pairs),
        in_specs=[
            pl.BlockSpec((1, n_ctx, LANES), lambda bi, p: (bi, 0, p)),
            pl.BlockSpec((1, n_ctx, LANES), lambda bi, p: (bi, 0, k_blk0 + p)),
            pl.BlockSpec((1, LANES, n_ctx), lambda bi, p: (bi, p, 0)),
        ],
        out_specs=pl.BlockSpec((1, n_ctx, LANES), lambda bi, p: (bi, 0, p)),
        out_shape=jax.ShapeDtypeStruct((b, n_ctx, n_pairs * LANES), BF16),
        compiler_params=_cparams(),
        name=name,
    )(qk_ctx, qk_ctx, vt_ctx)


def _na_bias_tables(rpb, seq):
    n_heads = rpb.shape[0]
    rows = seq // GRID_W
    tq = NA_ROWS_PER_BLOCK * GRID_W
    nj = seq // tq
    kh = min(NA_KH, rows)
    n_dc = rpb.shape[2]
    pad = GRID_W - 1 - (n_dc - 1) // 2
    rpb_pad = jnp.pad(rpb.astype(F32), ((0, 0), (0, 0), (pad, pad)), mode="edge")
    col_tab = jnp.stack([rpb_pad[:, :, GRID_W - 1 - qc:2 * GRID_W - 1 - qc] for qc in range(GRID_W)], axis=-1)
    tabs = []
    for j in (0, 1, nj - 1):
        ws = NA_ROWS_PER_BLOCK * int(np.clip(j - 1, 0, nj - 3))
        n_kr = 3 * NA_ROWS_PER_BLOCK
        kr = ws + np.arange(n_kr)
        r = NA_ROWS_PER_BLOCK * j + np.arange(NA_ROWS_PER_BLOCK)
        dr = np.clip(kr[:, None] - r[None, :] + NA_KH - 1, 0, 2 * NA_KH - 2)
        blocks = jnp.stack([jnp.stack([col_tab[:, dr[a, q]] for q in range(NA_ROWS_PER_BLOCK)], axis=1)
                            for a in range(n_kr)], axis=1)
        vals = blocks.transpose(0, 1, 3, 2, 4).reshape(n_heads, 3 * tq, tq)
        kk = np.arange(3 * tq)
        kr_f, kc = ws + kk // GRID_W, kk % GRID_W
        qq = np.arange(tq)
        r_f, cq = NA_ROWS_PER_BLOCK * j + qq // GRID_W, qq % GRID_W
        rs = np.clip(r_f - kh // 2, 0, rows - kh)
        cs = np.clip(cq - NA_KW // 2, 0, GRID_W - NA_KW)
        ok = ((kr_f[:, None] >= rs[None, :]) & (kr_f[:, None] < rs[None, :] + kh)
              & (kc[:, None] >= cs[None, :]) & (kc[:, None] < cs[None, :] + NA_KW))
        tabs.append(jnp.where(ok[None], vals, -jnp.inf))
    tab = jnp.stack(tabs)
    tab = tab.reshape(3, n_heads // 2, 2, 3 * tq, tq).transpose(0, 1, 3, 2, 4)
    return tab.reshape(3, n_heads // 2, 3 * tq, 2 * tq)


def _outproj_kernel(*refs, n_att):
    att_refs = refs[:n_att]
    x_ref, mod_ref, w_ref, lng_ref, lnb_ref, o_ref = refs[n_att:]
    y = None
    k0 = 0
    for a_ref in att_refs:
        kw = a_ref.shape[2]
        part = jnp.dot(a_ref[0], w_ref[k0:k0 + kw, :], preferred_element_type=F32)
        y = part if y is None else y + part
        k0 += kw
    z = ALPHA * x_ref[0] + mod_ref[0][2:3] * y
    o_ref[0] = _layer_norm(z, lng_ref[...], lnb_ref[...])


def _outproj_call(atts, x, mod, mod_row, w_o, ln_g, ln_b, tm, name):
    b, t, d = x.shape
    mod_map = (lambda bi, i: (bi, 0, 0)) if mod_row is None else (lambda bi, i: (mod_row, 0, 0))
    in_specs = [pl.BlockSpec((1, tm, a.shape[2]), lambda bi, i: (bi, i, 0)) for a in atts]
    in_specs += [
        pl.BlockSpec((1, tm, d), lambda bi, i: (bi, i, 0)),
        pl.BlockSpec((1,) + mod.shape[1:], mod_map),
        _const_spec(w_o.shape),
        _const_spec((1, d)),
        _const_spec((1, d)),
    ]
    return pl.pallas_call(
        functools.partial(_outproj_kernel, n_att=len(atts)),
        grid=(b, t // tm),
        in_specs=in_specs,
        out_specs=pl.BlockSpec((1, tm, d), lambda bi, i: (bi, i, 0)),
        out_shape=jax.ShapeDtypeStruct((b, t, d), F32),
        compiler_params=_cparams(),
        name=name,
    )(*atts, x, mod, w_o, ln_g.reshape(1, d), ln_b.reshape(1, d))


def _ffn_kernel(*refs, tm, halo):
    if halo:
        x_ref, xp_ref, xn_ref, mod_ref, wup_ref, cw_ref, cb_ref, wdn_ref, lng_ref, lnb_ref, o_ref, hs_ref, up_ref = refs
    else:
        x_ref, mod_ref, wup_ref, cw_ref, cb_ref, wdn_ref, lng_ref, lnb_ref, o_ref, hs_ref, up_ref = refs
    d = x_ref.shape[2]
    mod = mod_ref[0]
    shift, scale, gate = mod[3:4], mod[4:5], mod[5:6]
    x = x_ref[0]
    hs_ref[HALO:HALO + tm, :] = (x * (1.0 + scale) + shift).astype(BF16)
    zeros8 = jnp.zeros((SUBLANES_F32, d), F32)
    if halo:
        i = pl.program_id(1)
        last = pl.num_programs(1) - 1
        hp = jnp.where(i > 0, xp_ref[0] * (1.0 + scale) + shift, 0.0)
        hn = jnp.where(i < last, xn_ref[0] * (1.0 + scale) + shift, 0.0)
    else:
        hp = hn = zeros8
    hs_ref[0:HALO, :] = jnp.concatenate([zeros8, hp], axis=0).astype(BF16)
    hs_ref[HALO + tm:HALO + tm + HALO, :] = jnp.concatenate([hn, zeros8], axis=0).astype(BF16)
    hs = hs_ref[...]
    acc = None
    n_chunks = wdn_ref.shape[0] // FF_CHUNK
    for c in range(n_chunks):
        cols = slice(c * 2 * FF_CHUNK, (c + 1) * 2 * FF_CHUNK)
        up_ref[...] = jnp.dot(hs, wup_ref[:, cols], preferred_element_type=F32)
        w = cw_ref[:, cols]
        u = (up_ref[HALO - 1:HALO - 1 + tm, :] * w[0:1]
             + up_ref[HALO:HALO + tm, :] * w[1:2]
             + up_ref[HALO + 1:HALO + 1 + tm, :] * w[2:3]
             + cb_ref[:, cols])
        act = (_gelu_tanh(u[:, FF_CHUNK:]) * u[:, :FF_CHUNK]).astype(BF16)
        part = jnp.dot(act, wdn_ref[c * FF_CHUNK:(c + 1) * FF_CHUNK, :], preferred_element_type=F32)
        acc = part if acc is None else acc + part
    z = ALPHA * x + gate * acc
    o_ref[0] = _layer_norm(z, lng_ref[...], lnb_ref[...])


def _ffn_call(x, mod, mod_row, w_up, conv_w, conv_b, w_down, ln_g, ln_b, tm, name):
    b, t, d = x.shape
    halo = t > tm
    mod_map = (lambda bi, i: (bi, 0, 0)) if mod_row is None else (lambda bi, i: (mod_row, 0, 0))
    in_specs = [pl.BlockSpec((1, tm, d), lambda bi, i: (bi, i, 0))]
    args = [x]
    if halo:
        r8 = tm // SUBLANES_F32
        n8 = t // SUBLANES_F32
        in_specs += [
            pl.BlockSpec((1, SUBLANES_F32, d), lambda bi, i: (bi, jnp.maximum(i * r8 - 1, 0), 0)),
            pl.BlockSpec((1, SUBLANES_F32, d), lambda bi, i: (bi, jnp.minimum((i + 1) * r8, n8 - 1), 0)),
        ]
        args += [x, x]
    in_specs += [
        pl.BlockSpec((1,) + mod.shape[1:], mod_map),
        _const_spec(w_up.shape),
        _const_spec(conv_w.shape),
        _const_spec(conv_b.shape),
        _const_spec(w_down.shape),
        _const_spec((1, d)),
        _const_spec((1, d)),
    ]
    args += [mod, w_up, conv_w, conv_b, w_down, ln_g.reshape(1, d), ln_b.reshape(1, d)]
    return pl.pallas_call(
        functools.partial(_ffn_kernel, tm=tm, halo=halo),
        grid=(b, t // tm),
        in_specs=in_specs,
        out_specs=pl.BlockSpec((1, tm, d), lambda bi, i: (bi, i, 0)),
        out_shape=jax.ShapeDtypeStruct((b, t, d), F32),
        scratch_shapes=[
            pltpu.VMEM((tm + 2 * HALO, d), BF16),
            pltpu.VMEM((tm + 2 * HALO, 2 * FF_CHUNK), F32),
        ],
        compiler_params=_cparams(),
        name=name,
    )(*args)


def _rope_tables(seq):
    n_freq = HEAD_DIM // 4
    inv = ROPE_BASE ** (-jnp.arange(n_freq, dtype=F32) / n_freq)
    t = jnp.arange(seq)
    row = (t // GRID_W).astype(F32)
    col = (t % GRID_W).astype(F32)
    ar = row[:, None] * inv
    ac = col[:, None] * inv
    ang = jnp.concatenate([ar, ar, ac, ac], -1)
    cos, sin = jnp.cos(ang), jnp.sin(ang)
    even_chunk = ((np.arange(HEAD_DIM) // n_freq) % 2 == 0)[None, :]
    sin_a = jnp.where(even_chunk, -sin, 0.0)
    sin_b = jnp.where(even_chunk, 0.0, sin)
    return tuple(jnp.concatenate([a, a], axis=1) for a in (cos, sin_a, sin_b))


def _interleave_ff(a, d_ff):
    lead = a.shape[:-1]
    a = a.reshape(lead + (2, d_ff // FF_CHUNK, FF_CHUNK))
    a = jnp.swapaxes(a, -3, -2)
    return a.reshape(lead + (2 * d_ff,))


def _dup_heads(w, n_heads):
    d = w.shape[0]
    w = w.reshape(d, n_heads, HEAD_DIM)
    return jnp.concatenate([w, w], axis=-1).reshape(d, n_heads * PAIR)


def kernel(x, c, ctx, c_ctx, w_ada, b_ada, ln_g, ln_b, w_in_ab, w_o_ab, na_rpb, diff_lambda, diff_subln,
           w_in_c, w_o_c, gqa_qk_norm, w_up, conv_w, conv_b, w_down):
    b, s, d = x.shape
    n_ctx = ctx.shape[1]
    d_ff = w_down.shape[1]
    na_w = na_rpb.shape[1] * HEAD_DIM
    diff_w = d - na_w
    gqa_kw = GQA_KV_HEADS * HEAD_DIM
    tm_lat = min(TM_LATENT, s)

    mod_rows = -(-(b + 1) // SUBLANES_F32) * SUBLANES_F32
    ctx_row = b
    cc = jnp.concatenate([c, c_ctx[None, :], jnp.zeros((mod_rows - b - 1, d), F32)], axis=0)
    mod_all = _ada_call(cc, w_ada, b_ada)
    rope_tabs = _rope_tables(s)
    ones_gain = jnp.ones((2, LANES), F32)

    xc = ctx
    for l in range(DEPTH):
        need_ctx = l < DEPTH - 1
        i = l // 2
        mod = mod_all[l].reshape(mod_rows, 6, d)
        if l % 2 == 0:
            lambda_init = 0.8 - 0.6 * math.exp(-0.3 * l)
            w = w_in_ab[i]
            o0 = 3 * na_w
            wqk = jnp.concatenate([w[:, :2 * na_w], w[:, o0:o0 + 2 * diff_w]], axis=1).astype(BF16)
            wvt = jnp.concatenate([w[:, 2 * na_w:o0], w[:, o0 + 2 * diff_w:]], axis=1).T.astype(BF16)
            nb = na_w // LANES
            plan = ([(None, False, QK_SCALE)] * nb + [(None, False, 1.0)] * nb
                    + [(None, True, QK_SCALE)] * (diff_w // LANES) + [(None, True, 1.0)] * (diff_w // LANES))
            gains = ones_gain
        else:
            w = w_in_c[i]
            q_w = d
            wqk = jnp.concatenate([w[:, :q_w], _dup_heads(w[:, q_w:q_w + gqa_kw], GQA_KV_HEADS)],
                                  axis=1).astype(BF16)
            wvt = _dup_heads(w[:, q_w + gqa_kw:], GQA_KV_HEADS).T.astype(BF16)
            plan = [(0, True, QK_SCALE)] * (q_w // LANES) + [(1, True, 1.0)] * GQA_KV_HEADS
            gains = jnp.concatenate([gqa_qk_norm[i], gqa_qk_norm[i]], axis=1)
        plan = tuple(plan)

        qk_lat, vt_lat = _inproj_call(x, mod, None, wqk, wvt, gains, plan, rope_tabs, tm_lat, f"inproj_lat_{l}")
        qk_ctx, vt_ctx = _inproj_call(xc, mod, ctx_row, wqk, wvt, gains, plan, None, n_ctx, f"inproj_ctx_{l}")

        if l % 2 == 0:
            bias_tab = _na_bias_tables(na_rpb[i], s)
            atts = [
                _na_call(qk_lat, vt_lat, qk_ctx, vt_ctx, bias_tab, f"na_lat_{l}"),
                _diff_call(qk_lat, qk_ctx, vt_ctx, qk_lat, vt_lat, diff_lambda[i], diff_subln[i], lambda_init,
                           min(TQ_DIFF, s), f"diff_lat_{l}"),
            ]
            w_o = w_o_ab[i].astype(BF16)
            if need_ctx:
                atts_c = [
                    _na_ctx_call(qk_ctx, vt_ctx, na_w // LANES, f"na_ctx_{l}"),
                    _diff_call(qk_ctx, qk_ctx, vt_ctx, None, None, diff_lambda[i], diff_subln[i], lambda_init,
                               n_ctx, f"diff_ctx_{l}"),
                ]
        else:
            atts = [_gqa_call(qk_lat, qk_ctx, vt_ctx, qk_lat, vt_lat, min(TQ_GQA, s), f"gqa_lat_{l}")]
            w_o = w_o_c[i].astype(BF16)
            if need_ctx:
                atts_c = [_gqa_call(qk_ctx, qk_ctx, vt_ctx, None, None, n_ctx, f"gqa_ctx_{l}")]

        w_up_l = _interleave_ff(w_up[l], d_ff).astype(BF16)
        conv_w_l = _interleave_ff(conv_w[l], d_ff)
        conv_b_l = _interleave_ff(conv_b[l], d_ff).reshape(1, 2 * d_ff)
        w_down_l = w_down[l].astype(BF16)

        x = _outproj_call(atts, x, mod, None, w_o, ln_g[l, 0], ln_b[l, 0], tm_lat, f"outproj_lat_{l}")
        x = _ffn_call(x, mod, None, w_up_l, conv_w_l, conv_b_l, w_down_l, ln_g[l, 1], ln_b[l, 1],
                      tm_lat, f"ffn_lat_{l}")
        if need_ctx:
            xc = _outproj_call(atts_c, xc, mod, ctx_row, w_o, ln_g[l, 0], ln_b[l, 0], n_ctx, f"outproj_ctx_{l}")
            xc = _ffn_call(xc, mod, ctx_row, w_up_l, conv_w_l, conv_b_l, w_down_l, ln_g[l, 1], ln_b[l, 1],
                           n_ctx, f"ffn_ctx_{l}")
    return x
```

```python
import functools
import math

import numpy as np
import jax
import jax.numpy as jnp
from jax import lax
from jax.experimental import pallas as pl
from jax.experimental.pallas import tpu as pltpu

F32 = jnp.float32
BF16 = jnp.bfloat16

DEPTH = 4
GRID_W = 64
HEAD_DIM = 64
NA_KH = 8
NA_KW = 16
GQA_KV_HEADS = 4
CONV_WIDTH = 3
ROPE_BASE = 10000.0
EPS = 1e-6
ALPHA = (2 * DEPTH) ** 0.25
LOG2E = math.log2(math.e)
Q_SCALE = HEAD_DIM ** -0.5 * LOG2E

LANES = 128
SUBLANES_F32 = 8
SUBLANES_BF16 = 16
MXU_DIM = 256
VMEM_LIMIT_BYTES = 56 * 1024 * 1024

PAIR = 2 * HEAD_DIM
assert PAIR == LANES

TM_LATENT = 512
TQ_GQA = 256
TQ_DIFF = 512
NA_ROWS_PER_BLOCK = 4
KEY_CHUNK = MXU_DIM
ONES_ROWS = SUBLANES_BF16
FF_CHUNK = MXU_DIM
ADA_TN = 1536
HALO = SUBLANES_BF16
FFN_STAGE_LAG = 2


def _cparams():
    return pltpu.CompilerParams(vmem_limit_bytes=VMEM_LIMIT_BYTES)


def _const_spec(shape):
    nd = len(shape)
    return pl.BlockSpec(shape, lambda *_: (0,) * nd, pipeline_mode=pl.Buffered(1))


def _layer_norm(z, g, b):
    mu = jnp.mean(z, axis=-1, keepdims=True)
    d = z - mu
    var = jnp.mean(d * d, axis=-1, keepdims=True)
    return d * lax.rsqrt(var + EPS) * g + b


def _gelu_tanh(x):
    return 0.5 * x * (1.0 + jnp.tanh(math.sqrt(2.0 / math.pi) * (x + 0.044715 * (x * x * x))))


def _group_sum_matrix():
    r = lax.broadcasted_iota(jnp.int32, (2 * LANES, LANES), 0)
    c = lax.broadcasted_iota(jnp.int32, (2 * LANES, LANES), 1)
    same = ((r % LANES) // HEAD_DIM) == (c // HEAD_DIM)
    return jnp.where(same, 1.0, 0.0).astype(BF16)


def _head_rms(xb, gm, g):
    x2 = xb * xb
    hi = x2.astype(BF16)
    lo = (x2 - hi.astype(F32)).astype(BF16)
    ssq = jnp.dot(jnp.concatenate([hi, lo], axis=1), gm, preferred_element_type=F32)
    return xb * lax.rsqrt(ssq * (1.0 / HEAD_DIM) + EPS) * g


def _rope(xb, cos, sin_a, sin_b):
    return (xb * cos
            + pltpu.roll(xb, LANES - HEAD_DIM // 4, 1) * sin_a
            + pltpu.roll(xb, HEAD_DIM // 4, 1) * sin_b)


def _split_heads_rows(q_pair):
    qf = q_pair.astype(F32)
    lane = lax.broadcasted_iota(jnp.int32, qf.shape, 1)
    lo = lane < HEAD_DIM
    return [jnp.where(lo, qf, 0.0).astype(BF16), jnp.where(lo, 0.0, qf).astype(BF16)]


def _attend(streams):
    n_chunks = len(streams[0][1])
    s_buf, e_buf = {}, {}
    m = [None] * len(streams)
    acc = [None] * len(streams)
    for t in range(n_chunks + 2):
        for i, (q_stack, chunks) in enumerate(streams):
            if t < n_chunks:
                k_c, _, bias_c = chunks[t]
                s = lax.dot_general(k_c, q_stack, (((1,), (1,)), ((), ())), preferred_element_type=F32)
                s_buf[i, t] = s if bias_c is None else s + bias_c
        for i, (q_stack, chunks) in enumerate(streams):
            if 0 <= t - 1 < n_chunks:
                s = s_buf.pop((i, t - 1))
                mc = jnp.max(s, axis=0, keepdims=True)
                m_new = mc if m[i] is None else jnp.maximum(m[i], mc)
                alpha = None if m[i] is None else jnp.exp2(m[i] - m_new)
                m[i] = m_new
                e_buf[i, t - 1] = (jnp.exp2(s - m_new).astype(BF16), alpha)
        for i, (q_stack, chunks) in enumerate(streams):
            if 0 <= t - 2 < n_chunks:
                e, alpha = e_buf.pop((i, t - 2))
                vt_c = chunks[t - 2][1]
                vt_aug = jnp.concatenate([vt_c, jnp.ones((ONES_ROWS, vt_c.shape[1]), BF16)], axis=0)
                pv = jnp.dot(vt_aug, e, preferred_element_type=F32)
                acc[i] = pv if alpha is None else alpha * acc[i] + pv
    outs = []
    for (q_stack, chunks), a in zip(streams, acc):
        dv = chunks[0][1].shape[0]
        outs.append(a[:dv] * (1.0 / a[dv:dv + 1]))
    return outs


def _key_chunks(kc_ref, vc_ref, kl_ref, vl_ref, k_cols=slice(None), v_rows=slice(None)):
    chunks = [(kc_ref[0, :, k_cols], vc_ref[0, v_rows, :], None)]
    if kl_ref is not None:
        n_lat = kl_ref.shape[1]
        for c0 in range(0, n_lat, KEY_CHUNK):
            chunks.append((kl_ref[0, c0:c0 + KEY_CHUNK, k_cols], vl_ref[0, v_rows, c0:c0 + KEY_CHUNK], None))
    return chunks


def _pick_head_rows(res, tq, p):
    row = lax.broadcasted_iota(jnp.int32, (LANES, tq), 0)
    a = res[:, (2 * p) * tq:(2 * p + 1) * tq]
    b = res[:, (2 * p + 1) * tq:(2 * p + 2) * tq]
    return jnp.where(row < HEAD_DIM, a, b).T


def _stack_head_rows(res, tq, p):
    return jnp.concatenate([res[:, (2 * p) * tq:(2 * p + 1) * tq],
                            res[:, (2 * p + 1) * tq:(2 * p + 2) * tq]], axis=0).T


def _ada_kernel(c_ref, w_ref, b_ref, o_ref):
    c = c_ref[...]
    cond = c * (1.0 / (1.0 + jnp.exp(-c)))
    o_ref[0] = jnp.dot(cond, w_ref[0], precision=lax.Precision.HIGHEST,
                       preferred_element_type=F32) + b_ref[0]


def _ada_call(cc, w_ada, b_ada):
    n_layers, d, n_out = w_ada.shape
    rows = cc.shape[0]
    return pl.pallas_call(
        _ada_kernel,
        grid=(n_layers, n_out // ADA_TN),
        in_specs=[
            pl.BlockSpec((rows, d), lambda l, n: (0, 0)),
            pl.BlockSpec((1, d, ADA_TN), lambda l, n: (l, 0, n)),
            pl.BlockSpec((1, 1, ADA_TN), lambda l, n: (l, 0, n)),
        ],
        out_specs=pl.BlockSpec((1, rows, ADA_TN), lambda l, n: (l, 0, n)),
        out_shape=jax.ShapeDtypeStruct((n_layers, rows, n_out), F32),
        compiler_params=_cparams(),
        name="adaln_mod",
    )(cc, w_ada, b_ada.reshape(n_layers, 1, n_out))


def _inproj_kernel(*refs, block_plan, rope):
    if rope:
        x_ref, mod_ref, wqk_ref, wvt_ref, g_ref, cos_ref, sa_ref, sb_ref, oqk_ref, ovt_ref = refs
    else:
        x_ref, mod_ref, wqk_ref, wvt_ref, g_ref, oqk_ref, ovt_ref = refs
    mod = mod_ref[0]
    h = (x_ref[0] * (1.0 + mod[1:2]) + mod[0:1]).astype(BF16)
    qk = jnp.dot(h, wqk_ref[...], preferred_element_type=F32)
    vt = lax.dot_general(wvt_ref[...], h, (((1,), (1,)), ((), ())), preferred_element_type=F32)
    ovt_ref[0] = vt.astype(BF16)
    gm = _group_sum_matrix() if any(p[0] is not None for p in block_plan) else None
    if rope:
        cos, sa, sb = cos_ref[...], sa_ref[...], sb_ref[...]
    for j, (gain_row, do_rope, scale) in enumerate(block_plan):
        blk = qk[:, j * LANES:(j + 1) * LANES]
        if gain_row is not None:
            blk = _head_rms(blk, gm, g_ref[gain_row:gain_row + 1, :])
        if do_rope and rope:
            blk = _rope(blk, cos, sa, sb)
        if scale != 1.0:
            blk = blk * scale
        oqk_ref[0, :, j * LANES:(j + 1) * LANES] = blk.astype(BF16)


def _inproj_call(x, mod, mod_row, wqk, wvt, gains, block_plan, rope_tabs, tm, name):
    b, t, d = x.shape
    wq = wqk.shape[1]
    wv = wvt.shape[0]
    rope = rope_tabs is not None
    mod_map = (lambda bi, i: (bi, 0, 0)) if mod_row is None else (lambda bi, i: (mod_row, 0, 0))
    in_specs = [
        pl.BlockSpec((1, tm, d), lambda bi, i: (bi, i, 0)),
        pl.BlockSpec((1,) + mod.shape[1:], mod_map),
        _const_spec(wqk.shape),
        _const_spec(wvt.shape),
        _const_spec(gains.shape),
    ]
    args = [x, mod, wqk, wvt, gains]
    if rope:
        in_specs += [pl.BlockSpec((tm, LANES), lambda bi, i: (i, 0))] * 3
        args += list(rope_tabs)
    return pl.pallas_call(
        functools.partial(_inproj_kernel, block_plan=block_plan, rope=rope),
        grid=(b, t // tm),
        in_specs=in_specs,
        out_specs=[
            pl.BlockSpec((1, tm, wq), lambda bi, i: (bi, i, 0)),
            pl.BlockSpec((1, wv, tm), lambda bi, i: (bi, 0, i)),
        ],
        out_shape=[jax.ShapeDtypeStruct((b, t, wq), BF16), jax.ShapeDtypeStruct((b, wv, t), BF16)],
        compiler_params=_cparams(),
        name=name,
    )(*args)


def _gqa_kernel(*refs, tq, latent_keys):
    if latent_keys:
        q_ref, kc_ref, vc_ref, kl_ref, vl_ref, o_ref = refs
    else:
        q_ref, kc_ref, vc_ref, o_ref = refs
        kl_ref = vl_ref = None
    q = q_ref[0]
    parts = _split_heads_rows(q[:, :LANES]) + _split_heads_rows(q[:, LANES:])
    (res,) = _attend([(jnp.concatenate(parts, axis=0), _key_chunks(kc_ref, vc_ref, kl_ref, vl_ref))])
    o_ref[0] = jnp.concatenate([_stack_head_rows(res, tq, 0), _stack_head_rows(res, tq, 1)],
                               axis=1).astype(BF16)


def _gqa_call(qk_q, qk_ctx, vt_ctx, qk_lat, vt_lat, tq, name):
    b, t, _ = qk_q.shape
    n_ctx = qk_ctx.shape[1]
    k_blk0 = (GQA_KV_HEADS * 4 * HEAD_DIM) // LANES
    latent_keys = qk_lat is not None
    in_specs = [
        pl.BlockSpec((1, tq, 2 * LANES), lambda bi, g, i: (bi, i, g)),
        pl.BlockSpec((1, n_ctx, LANES), lambda bi, g, i: (bi, 0, k_blk0 + g)),
        pl.BlockSpec((1, HEAD_DIM, n_ctx), lambda bi, g, i: (bi, g, 0)),
    ]
    args = [qk_q, qk_ctx, vt_ctx]
    if latent_keys:
        n_lat = qk_lat.shape[1]
        in_specs += [
            pl.BlockSpec((1, n_lat, LANES), lambda bi, g, i: (bi, 0, k_blk0 + g)),
            pl.BlockSpec((1, HEAD_DIM, n_lat), lambda bi, g, i: (bi, g, 0)),
        ]
        args += [qk_lat, vt_lat]
    return pl.pallas_call(
        functools.partial(_gqa_kernel, tq=tq, latent_keys=latent_keys),
        grid=(b, GQA_KV_HEADS, t // tq),
        in_specs=in_specs,
        out_specs=pl.BlockSpec((1, tq, 2 * LANES), lambda bi, g, i: (bi, i, g)),
        out_shape=jax.ShapeDtypeStruct((b, t, GQA_KV_HEADS * 2 * LANES), BF16),
        compiler_params=_cparams(),
        name=name,
    )(*args)


def _diff_kernel(*refs, tq, latent_keys, lambda_init):
    if latent_keys:
        q_ref, kc_ref, vc_ref, kl_ref, vl_ref, lam_ref, g_ref, o_ref = refs
    else:
        q_ref, kc_ref, vc_ref, lam_ref, g_ref, o_ref = refs
        kl_ref = vl_ref = None
    q_stack = jnp.concatenate(_split_heads_rows(q_ref[0]), axis=0)
    (res,) = _attend([(q_stack, _key_chunks(kc_ref, vc_ref, kl_ref, vl_ref))])
    lv = lam_ref[...]
    lam = (jnp.exp(jnp.sum(lv[0:1] * lv[1:2], axis=1, keepdims=True))
           - jnp.exp(jnp.sum(lv[2:3] * lv[3:4], axis=1, keepdims=True)) + lambda_init)
    o = (res[:, :tq] - lam * res[:, tq:]).T
    o = o * lax.rsqrt(jnp.mean(o * o, axis=-1, keepdims=True) + EPS) * g_ref[...]
    o_ref[0] = (o * (1.0 - lambda_init)).astype(BF16)


def _diff_call(qk_q, qk_ctx, vt_ctx, qk_lat, vt_lat, lam_vec, subln_g, lambda_init, tq, name):
    b, t, wqk = qk_q.shape
    n_ctx = qk_ctx.shape[1]
    n_heads = wqk // (4 * LANES)
    q_blk0, k_blk0, v_blk0 = 2 * n_heads, 3 * n_heads, n_heads
    latent_keys = qk_lat is not None
    in_specs = [
        pl.BlockSpec((1, tq, LANES), lambda bi, h, i: (bi, i, q_blk0 + h)),
        pl.BlockSpec((1, n_ctx, LANES), lambda bi, h, i: (bi, 0, k_blk0 + h)),
        pl.BlockSpec((1, LANES, n_ctx), lambda bi, h, i: (bi, v_blk0 + h, 0)),
    ]
    args = [qk_q, qk_ctx, vt_ctx]
    if latent_keys:
        n_lat = qk_lat.shape[1]
        in_specs += [
            pl.BlockSpec((1, n_lat, LANES), lambda bi, h, i: (bi, 0, k_blk0 + h)),
            pl.BlockSpec((1, LANES, n_lat), lambda bi, h, i: (bi, v_blk0 + h, 0)),
        ]
        args += [qk_lat, vt_lat]
    in_specs += [_const_spec(lam_vec.shape), _const_spec((1, LANES))]
    args += [lam_vec, subln_g.reshape(1, LANES)]
    return pl.pallas_call(
        functools.partial(_diff_kernel, tq=tq, latent_keys=latent_keys, lambda_init=lambda_init),
        grid=(b, n_heads, t // tq),
        in_specs=in_specs,
        out_specs=pl.BlockSpec((1, tq, LANES), lambda bi, h, i: (bi, i, h)),
        out_shape=jax.ShapeDtypeStruct((b, t, n_heads * LANES), BF16),
        compiler_params=_cparams(),
        name=name,
    )(*args)


def _na_kernel(*refs, n_local):
    q_ref, kc_ref, vc_ref = refs[:3]
    k_refs = refs[3:3 + n_local]
    v_refs = refs[3 + n_local:3 + 2 * n_local]
    bias_ref = refs[3 + 2 * n_local] if n_local else None
    o_ref = refs[-1]
    tq = q_ref.shape[1]
    n_pairs = q_ref.shape[2] // LANES
    streams = []
    for p in range(n_pairs):
        cols = slice(p * LANES, (p + 1) * LANES)
        q_stack = jnp.concatenate(_split_heads_rows(q_ref[0, :, cols]), axis=0)
        chunks = [(kc_ref[0, :, cols], vc_ref[0, cols, :], None)]
        for c in range(n_local):
            chunks.append((k_refs[c][0, :, cols], v_refs[c][0, cols, :], bias_ref[0, p, c * tq:(c + 1) * tq, :]))
        streams.append((q_stack, chunks))
    for p, res in enumerate(_attend(streams)):
        o_ref[0, :, p * LANES:(p + 1) * LANES] = _pick_head_rows(res, tq, 0).astype(BF16)


def _na_call(qk_lat, vt_lat, qk_ctx, vt_ctx, bias_tab, name):
    b, s, _ = qk_lat.shape
    n_ctx = qk_ctx.shape[1]
    tq = NA_ROWS_PER_BLOCK * GRID_W
    nj = s // tq
    n_pairs = bias_tab.shape[1]
    w = n_pairs * LANES
    n_local = 3

    def tb(j):
        return jnp.clip(j - 1, 0, nj - n_local)

    def cls(j):
        return jnp.where(j == 0, 0, jnp.where(j == nj - 1, 2, 1))

    k_specs = [pl.BlockSpec((1, tq, w), functools.partial(lambda j, bi, d: (bi, tb(j) + d, 1), d=d))
               for d in range(n_local)]
    v_specs = [pl.BlockSpec((1, w, tq), functools.partial(lambda j, bi, d: (bi, 0, tb(j) + d), d=d))
               for d in range(n_local)]
    return pl.pallas_call(
        functools.partial(_na_kernel, n_local=n_local),
        grid=(nj, b),
        in_specs=[
            pl.BlockSpec((1, tq, w), lambda j, bi: (bi, j, 0)),
            pl.BlockSpec((1, n_ctx, w), lambda j, bi: (bi, 0, 1)),
            pl.BlockSpec((1, w, n_ctx), lambda j, bi: (bi, 0, 0)),
            *k_specs, *v_specs,
            pl.BlockSpec((1, n_pairs, n_local * tq, 2 * tq), lambda j, bi: (cls(j), 0, 0, 0)),
        ],
        out_specs=pl.BlockSpec((1, tq, w), lambda j, bi: (bi, j, 0)),
        out_shape=jax.ShapeDtypeStruct((b, s, w), BF16),
        compiler_params=_cparams(),
        name=name,
    )(qk_lat, qk_ctx, vt_ctx, *([qk_lat] * n_local), *([vt_lat] * n_local), bias_tab)


def _na_ctx_call(qk_ctx, vt_ctx, n_pairs, name):
    b, n_ctx, _ = qk_ctx.shape
    w = n_pairs * LANES
    return pl.pallas_call(
        functools.partial(_na_kernel, n_local=0),
        grid=(b,),
        in_specs=[
            pl.BlockSpec((1, n_ctx, w), lambda bi: (bi, 0, 0)),
            pl.BlockSpec((1, n_ctx, w), lambda bi: (bi, 0, 1)),
            pl.BlockSpec((1, w, n_ctx), lambda bi: (bi, 0, 0)),
        ],
        out_specs=pl.BlockSpec((1, n_ctx, w), lambda bi: (bi, 0, 0)),
        out_shape=jax.ShapeDtypeStruct((b, n_ctx, w), BF16),
        compiler_params=_cparams(),
        name=name,
    )(qk_ctx, qk_ctx, vt_ctx)


def _na_bias_tables(rpb, seq):
    n_heads = rpb.shape[0]
    rows = seq // GRID_W
    tq = NA_ROWS_PER_BLOCK * GRID_W
    nj = seq // tq
    kh = min(NA_KH, rows)
    n_dc = rpb.shape[2]
    pad = GRID_W - 1 - (n_dc - 1) // 2
    rpb_pad = jnp.pad(rpb.astype(F32), ((0, 0), (0, 0), (pad, pad)), mode="edge")
    col_tab = jnp.stack([rpb_pad[:, :, GRID_W - 1 - qc:2 * GRID_W - 1 - qc] for qc in range(GRID_W)], axis=-1)
    tabs = []
    for j in (0, 1, nj - 1):
        ws = NA_ROWS_PER_BLOCK * int(np.clip(j - 1, 0, nj - 3))
        n_kr = 3 * NA_ROWS_PER_BLOCK
        kr = ws + np.arange(n_kr)
        r = NA_ROWS_PER_BLOCK * j + np.arange(NA_ROWS_PER_BLOCK)
        dr = np.clip(kr[:, None] - r[None, :] + NA_KH - 1, 0, 2 * NA_KH - 2)
        blocks = jnp.stack([jnp.stack([col_tab[:, dr[a, q]] for q in range(NA_ROWS_PER_BLOCK)], axis=1)
                            for a in range(n_kr)], axis=1)
        vals = blocks.transpose(0, 1, 3, 2, 4).reshape(n_heads, 3 * tq, tq)
        kk = np.arange(3 * tq)
        kr_f, kc = ws + kk // GRID_W, kk % GRID_W
        qq = np.arange(tq)
        r_f, cq = NA_ROWS_PER_BLOCK * j + qq // GRID_W, qq % GRID_W
        rs = np.clip(r_f - kh // 2, 0, rows - kh)
        cs = np.clip(cq - NA_KW // 2, 0, GRID_W - NA_KW)
        ok = ((kr_f[:, None] >= rs[None, :]) & (kr_f[:, None] < rs[None, :] + kh)
              & (kc[:, None] >= cs[None, :]) & (kc[:, None] < cs[None, :] + NA_KW))
        tabs.append(jnp.where(ok[None], vals, -jnp.inf))
    tab = jnp.stack(tabs)
    tab = tab.reshape(3, n_heads // 2, 2, 3 * tq, tq).transpose(0, 1, 3, 2, 4)
    return tab.reshape(3, n_heads // 2, 3 * tq, 2 * tq)


def _outproj_kernel(*refs, n_att):
    att_refs = refs[:n_att]
    x_ref, mod_ref, w_ref, lng_ref, lnb_ref, o_ref = refs[n_att:]
    y = None
    k0 = 0
    for a_ref in att_refs:
        kw = a_ref.shape[2]
        part = jnp.dot(a_ref[0], w_ref[k0:k0 + kw, :], preferred_element_type=F32)
        y = part if y is None else y + part
        k0 += kw
    z = ALPHA * x_ref[0] + mod_ref[0][2:3] * y
    o_ref[0] = _layer_norm(z, lng_ref[...], lnb_ref[...])


def _outproj_call(atts, x, mod, mod_row, w_o, ln_g, ln_b, tm, name):
    b, t, d = x.shape
    mod_map = (lambda bi, i: (bi, 0, 0)) if mod_row is None else (lambda bi, i: (mod_row, 0, 0))
    in_specs = [pl.BlockSpec((1, tm, a.shape[2]), lambda bi, i: (bi, i, 0)) for a in atts]
    in_specs += [
        pl.BlockSpec((1, tm, d), lambda bi, i: (bi, i, 0)),
        pl.BlockSpec((1,) + mod.shape[1:], mod_map),
        _const_spec(w_o.shape),
        _const_spec((1, d)),
        _const_spec((1, d)),
    ]
    return pl.pallas_call(
        functools.partial(_outproj_kernel, n_att=len(atts)),
        grid=(b, t // tm),
        in_specs=in_specs,
        out_specs=pl.BlockSpec((1, tm, d), lambda bi, i: (bi, i, 0)),
        out_shape=jax.ShapeDtypeStruct((b, t, d), F32),
        compiler_params=_cparams(),
        name=name,
    )(*atts, x, mod, w_o, ln_g.reshape(1, d), ln_b.reshape(1, d))


def _ffn_kernel(*refs, tm, halo):
    if halo:
        x_ref, xp_ref, xn_ref, mod_ref, wup_ref, cw_ref, cb_ref, wdn_ref, lng_ref, lnb_ref, o_ref, hs_ref = refs
    else:
        x_ref, mod_ref, wup_ref, cw_ref, cb_ref, wdn_ref, lng_ref, lnb_ref, o_ref, hs_ref = refs
    d = x_ref.shape[2]
    rows = tm + HALO
    mod = mod_ref[0]
    shift, scale, gate = mod[3:4], mod[4:5], mod[5:6]
    x = x_ref[0]
    hs_ref[0:tm, :] = (x * (1.0 + scale) + shift).astype(BF16)
    zeros8 = jnp.zeros((SUBLANES_F32, d), F32)
    if halo:
        i = pl.program_id(1)
        last = pl.num_programs(1) - 1
        hp = jnp.where(i > 0, xp_ref[0] * (1.0 + scale) + shift, 0.0)
        hn = jnp.where(i < last, xn_ref[0] * (1.0 + scale) + shift, 0.0)
    else:
        hp = hn = zeros8
    hs_ref[tm:rows, :] = jnp.concatenate([hn, hp], axis=0).astype(BF16)
    hs = hs_ref[...]
    n_chunks = wdn_ref.shape[0] // FF_CHUNK

    def up_proj(c):
        return jnp.dot(hs, wup_ref[:, c * 2 * FF_CHUNK:(c + 1) * 2 * FF_CHUNK], preferred_element_type=F32)

    def conv_act(c, up):
        cols = slice(c * 2 * FF_CHUNK, (c + 1) * 2 * FF_CHUNK)
        w = cw_ref[:, cols]
        u = (pltpu.roll(up, 1, 0)[0:tm] * w[0:1] + up[0:tm] * w[1:2]
             + pltpu.roll(up, rows - 1, 0)[0:tm] * w[2:3] + cb_ref[:, cols])
        return (_gelu_tanh(u[:, FF_CHUNK:]) * u[:, :FF_CHUNK]).astype(BF16)

    ups, acts = {}, {}
    acc = None
    for t in range(n_chunks + 2 * FFN_STAGE_LAG):
        if t < n_chunks:
            ups[t] = up_proj(t)
        c = t - FFN_STAGE_LAG
        if 0 <= c < n_chunks:
            acts[c] = conv_act(c, ups.pop(c))
        c = t - 2 * FFN_STAGE_LAG
        if 0 <= c < n_chunks:
            part = jnp.dot(acts.pop(c), wdn_ref[c * FF_CHUNK:(c + 1) * FF_CHUNK, :], preferred_element_type=F32)
            acc = part if acc is None else acc + part
    z = ALPHA * x + gate * acc
    o_ref[0] = _layer_norm(z, lng_ref[...], lnb_ref[...])


def _ffn_call(x, mod, mod_row, w_up, conv_w, conv_b, w_down, ln_g, ln_b, tm, name):
    b, t, d = x.shape
    halo = t > tm
    mod_map = (lambda bi, i: (bi, 0, 0)) if mod_row is None else (lambda bi, i: (mod_row, 0, 0))
    in_specs = [pl.BlockSpec((1, tm, d), lambda bi, i: (bi, i, 0))]
    args = [x]
    if halo:
        r8 = tm // SUBLANES_F32
        n8 = t // SUBLANES_F32
        in_specs += [
            pl.BlockSpec((1, SUBLANES_F32, d), lambda bi, i: (bi, jnp.maximum(i * r8 - 1, 0), 0)),
            pl.BlockSpec((1, SUBLANES_F32, d), lambda bi, i: (bi, jnp.minimum((i + 1) * r8, n8 - 1), 0)),
        ]
        args += [x, x]
    in_specs += [
        pl.BlockSpec((1,) + mod.shape[1:], mod_map),
        _const_spec(w_up.shape),
        _const_spec(conv_w.shape),
        _const_spec(conv_b.shape),
        _const_spec(w_down.shape),
        _const_spec((1, d)),
        _const_spec((1, d)),
    ]
    args += [mod, w_up, conv_w, conv_b, w_down, ln_g.reshape(1, d), ln_b.reshape(1, d)]
    return pl.pallas_call(
        functools.partial(_ffn_kernel, tm=tm, halo=halo),
        grid=(b, t // tm),
        in_specs=in_specs,
        out_specs=pl.BlockSpec((1, tm, d), lambda bi, i: (bi, i, 0)),
        out_shape=jax.ShapeDtypeStruct((b, t, d), F32),
        scratch_shapes=[pltpu.VMEM((tm + HALO, d), BF16)],
        compiler_params=_cparams(),
        name=name,
    )(*args)


def _rope_tables(seq):
    n_freq = HEAD_DIM // 4
    inv = ROPE_BASE ** (-jnp.arange(n_freq, dtype=F32) / n_freq)
    t = jnp.arange(seq)
    row = (t // GRID_W).astype(F32)
    col = (t % GRID_W).astype(F32)
    ar = row[:, None] * inv
    ac = col[:, None] * inv
    ang = jnp.concatenate([ar, ar, ac, ac], -1)
    cos, sin = jnp.cos(ang), jnp.sin(ang)
    even_chunk = ((np.arange(HEAD_DIM) // n_freq) % 2 == 0)[None, :]
    sin_a = jnp.where(even_chunk, -sin, 0.0)
    sin_b = jnp.where(even_chunk, 0.0, sin)
    return tuple(jnp.concatenate([a, a], axis=1) for a in (cos, sin_a, sin_b))


def _interleave_ff(a, d_ff):
    lead = a.shape[:-1]
    a = a.reshape(lead + (2, d_ff // FF_CHUNK, FF_CHUNK))
    a = jnp.swapaxes(a, -3, -2)
    return a.reshape(lead + (2 * d_ff,))


def _dup_heads(w, n_heads):
    d = w.shape[0]
    w = w.reshape(d, n_heads, HEAD_DIM)
    return jnp.concatenate([w, w], axis=-1).reshape(d, n_heads * PAIR)


def kernel(x, c, ctx, c_ctx, w_ada, b_ada, ln_g, ln_b, w_in_ab, w_o_ab, na_rpb, diff_lambda, diff_subln,
           w_in_c, w_o_c, gqa_qk_norm, w_up, conv_w, conv_b, w_down):
    b, s, d = x.shape
    n_ctx = ctx.shape[1]
    d_ff = w_down.shape[1]
    na_w = na_rpb.shape[1] * HEAD_DIM
    diff_w = d - na_w
    gqa_kw = GQA_KV_HEADS * HEAD_DIM
    tm_lat = min(TM_LATENT, s)

    mod_rows = -(-(b + 1) // SUBLANES_F32) * SUBLANES_F32
    ctx_row = b
    cc = jnp.concatenate([c, c_ctx[None, :], jnp.zeros((mod_rows - b - 1, d), F32)], axis=0)
    mod_all = _ada_call(cc, w_ada, b_ada)
    rope_tabs = _rope_tables(s)
    ones_gain = jnp.ones((2, LANES), F32)

    xc = ctx
    for l in range(DEPTH):
        need_ctx = l < DEPTH - 1
        i = l // 2
        mod = mod_all[l].reshape(mod_rows, 6, d)
        if l % 2 == 0:
            lambda_init = 0.8 - 0.6 * math.exp(-0.3 * l)
            w = w_in_ab[i]
            o0 = 3 * na_w
            wqk = jnp.concatenate([w[:, :2 * na_w], w[:, o0:o0 + 2 * diff_w]], axis=1).astype(BF16)
            wvt = jnp.concatenate([w[:, 2 * na_w:o0], w[:, o0 + 2 * diff_w:]], axis=1).T.astype(BF16)
            nb = na_w // LANES
            plan = ([(None, False, Q_SCALE)] * nb + [(None, False, 1.0)] * nb
                    + [(None, True, Q_SCALE)] * (diff_w // LANES) + [(None, True, 1.0)] * (diff_w // LANES))
            gains = ones_gain
        else:
            w = w_in_c[i]
            q_w = d
            wqk = jnp.concatenate([w[:, :q_w], _dup_heads(w[:, q_w:q_w + gqa_kw], GQA_KV_HEADS)],
                                  axis=1).astype(BF16)
            wvt = w[:, q_w + gqa_kw:].T.astype(BF16)
            plan = [(0, True, Q_SCALE)] * (q_w // LANES) + [(1, True, 1.0)] * GQA_KV_HEADS
            gains = jnp.concatenate([gqa_qk_norm[i], gqa_qk_norm[i]], axis=1)
        plan = tuple(plan)

        qk_lat, vt_lat = _inproj_call(x, mod, None, wqk, wvt, gains, plan, rope_tabs, tm_lat, f"inproj_lat_{l}")
        qk_ctx, vt_ctx = _inproj_call(xc, mod, ctx_row, wqk, wvt, gains, plan, None, n_ctx, f"inproj_ctx_{l}")

        if l % 2 == 0:
            bias_tab = _na_bias_tables(na_rpb[i], s) * LOG2E
            atts = [
                _na_call(qk_lat, vt_lat, qk_ctx, vt_ctx, bias_tab, f"na_lat_{l}"),
                _diff_call(qk_lat, qk_ctx, vt_ctx, qk_lat, vt_lat, diff_lambda[i], diff_subln[i], lambda_init,
                           min(TQ_DIFF, s), f"diff_lat_{l}"),
            ]
            w_o = w_o_ab[i].astype(BF16)
            if need_ctx:
                atts_c = [
                    _na_ctx_call(qk_ctx, vt_ctx, na_w // LANES, f"na_ctx_{l}"),
                    _diff_call(qk_ctx, qk_ctx, vt_ctx, None, None, diff_lambda[i], diff_subln[i], lambda_init,
                               n_ctx, f"diff_ctx_{l}"),
                ]
        else:
            atts = [_gqa_call(qk_lat, qk_ctx, vt_ctx, qk_lat, vt_lat, min(TQ_GQA, s), f"gqa_lat_{l}")]
            w_o = w_o_c[i].astype(BF16)
            if need_ctx:
                atts_c = [_gqa_call(qk_ctx, qk_ctx, vt_ctx, None, None, n_ctx, f"gqa_ctx_{l}")]

        w_up_l = _interleave_ff(w_up[l], d_ff).astype(BF16)
        conv_w_l = _interleave_ff(conv_w[l], d_ff)
        conv_b_l = _interleave_ff(conv_b[l], d_ff).reshape(1, 2 * d_ff)
        w_down_l = w_down[l].astype(BF16)

        x = _outproj_call(atts, x, mod, None, w_o, ln_g[l, 0], ln_b[l, 0], tm_lat, f"outproj_lat_{l}")
        x = _ffn_call(x, mod, None, w_up_l, conv_w_l, conv_b_l, w_down_l, ln_g[l, 1], ln_b[l, 1],
                      tm_lat, f"ffn_lat_{l}")
        if need_ctx:
            xc = _outproj_call(atts_c, xc, mod, ctx_row, w_o, ln_g[l, 0], ln_b[l, 0], n_ctx, f"outproj_ctx_{l}")
            xc = _ffn_call(xc, mod, ctx_row, w_up_l, conv_w_l, conv_b_l, w_down_l, ln_g[l, 1], ln_b[l, 1],
                           n_ctx, f"ffn_ctx_{l}")
    return x
```

```python
import functools
import math

import numpy as np
import jax
import jax.numpy as jnp
from jax import lax
from jax.experimental import pallas as pl
from jax.experimental.pallas import tpu as pltpu

F32 = jnp.float32
BF16 = jnp.bfloat16

DEPTH = 4
GRID_W = 64
HEAD_DIM = 64
NA_KH = 8
NA_KW = 16
GQA_KV_HEADS = 4
CONV_WIDTH = 3
ROPE_BASE = 10000.0
EPS = 1e-6
ALPHA = (2 * DEPTH) ** 0.25
LOG2E = math.log2(math.e)
Q_SCALE = HEAD_DIM ** -0.5 * LOG2E

LANES = 128
SUBLANES_F32 = 8
SUBLANES_BF16 = 16
MXU_DIM = 256
VMEM_LIMIT_BYTES = 56 * 1024 * 1024

PAIR = 2 * HEAD_DIM
assert PAIR == LANES

TM_LATENT = 512
TQ_GQA = 512
TQ_DIFF = 512
NA_ROWS_PER_BLOCK = 4
KEY_CHUNK = MXU_DIM
ONES_ROWS = SUBLANES_BF16
FF_CHUNK = MXU_DIM
ADA_TN = 1536
HALO = SUBLANES_BF16


def _cparams():
    return pltpu.CompilerParams(vmem_limit_bytes=VMEM_LIMIT_BYTES)


def _const_spec(shape):
    nd = len(shape)
    return pl.BlockSpec(shape, lambda *_: (0,) * nd, pipeline_mode=pl.Buffered(1))


def _layer_norm(z, g, b):
    mu = jnp.mean(z, axis=-1, keepdims=True)
    d = z - mu
    var = jnp.mean(d * d, axis=-1, keepdims=True)
    return d * lax.rsqrt(var + EPS) * g + b


def _gelu_tanh(x):
    return 0.5 * x * (1.0 + jnp.tanh(math.sqrt(2.0 / math.pi) * (x + 0.044715 * (x * x * x))))


def _group_sum_matrix():
    r = lax.broadcasted_iota(jnp.int32, (2 * LANES, LANES), 0)
    c = lax.broadcasted_iota(jnp.int32, (2 * LANES, LANES), 1)
    same = ((r % LANES) // HEAD_DIM) == (c // HEAD_DIM)
    return jnp.where(same, 1.0, 0.0).astype(BF16)


def _head_rms(xb, gm, g):
    x2 = xb * xb
    hi = x2.astype(BF16)
    lo = (x2 - hi.astype(F32)).astype(BF16)
    ssq = jnp.dot(jnp.concatenate([hi, lo], axis=1), gm, preferred_element_type=F32)
    return xb * lax.rsqrt(ssq * (1.0 / HEAD_DIM) + EPS) * g


def _rope(xb, cos, sin_a, sin_b):
    return (xb * cos
            + pltpu.roll(xb, LANES - HEAD_DIM // 4, 1) * sin_a
            + pltpu.roll(xb, HEAD_DIM // 4, 1) * sin_b)


def _split_heads_rows(q_pair):
    qf = q_pair.astype(F32)
    lane = lax.broadcasted_iota(jnp.int32, qf.shape, 1)
    lo = lane < HEAD_DIM
    return [jnp.where(lo, qf, 0.0).astype(BF16), jnp.where(lo, 0.0, qf).astype(BF16)]


def _attend(streams):
    n_chunks = len(streams[0][1])
    s_buf, e_buf = {}, {}
    m = [None] * len(streams)
    acc = [None] * len(streams)
    for t in range(n_chunks + 2):
        for i, (q_stack, chunks) in enumerate(streams):
            if t < n_chunks:
                k_c, _, bias_c = chunks[t]
                s = lax.dot_general(k_c, q_stack, (((1,), (1,)), ((), ())), preferred_element_type=F32)
                s_buf[i, t] = s if bias_c is None else s + bias_c
        for i, (q_stack, chunks) in enumerate(streams):
            if 0 <= t - 1 < n_chunks:
                s = s_buf.pop((i, t - 1))
                mc = jnp.max(s, axis=0, keepdims=True)
                m_new = mc if m[i] is None else jnp.maximum(m[i], mc)
                alpha = None if m[i] is None else jnp.exp2(m[i] - m_new)
                m[i] = m_new
                e_buf[i, t - 1] = (jnp.exp2(s - m_new).astype(BF16), alpha)
        for i, (q_stack, chunks) in enumerate(streams):
            if 0 <= t - 2 < n_chunks:
                e, alpha = e_buf.pop((i, t - 2))
                vt_c = chunks[t - 2][1]
                vt_aug = jnp.concatenate([vt_c, jnp.ones((ONES_ROWS, vt_c.shape[1]), BF16)], axis=0)
                pv = jnp.dot(vt_aug, e, preferred_element_type=F32)
                acc[i] = pv if alpha is None else alpha * acc[i] + pv
    outs = []
    for (q_stack, chunks), a in zip(streams, acc):
        dv = chunks[0][1].shape[0]
        outs.append(a[:dv] * (1.0 / a[dv:dv + 1]))
    return outs


def _key_chunks(kc_ref, vc_ref, kl_ref, vl_ref, k_cols=slice(None), v_rows=slice(None)):
    chunks = [(kc_ref[0, :, k_cols], vc_ref[0, v_rows, :], None)]
    if kl_ref is not None:
        n_lat = kl_ref.shape[1]
        for c0 in range(0, n_lat, KEY_CHUNK):
            chunks.append((kl_ref[0, c0:c0 + KEY_CHUNK, k_cols], vl_ref[0, v_rows, c0:c0 + KEY_CHUNK], None))
    return chunks


def _pick_head_rows(res, tq, p):
    row = lax.broadcasted_iota(jnp.int32, (LANES, tq), 0)
    a = res[:, (2 * p) * tq:(2 * p + 1) * tq]
    b = res[:, (2 * p + 1) * tq:(2 * p + 2) * tq]
    return jnp.where(row < HEAD_DIM, a, b).T


def _stack_head_rows(res, tq, p):
    return jnp.concatenate([res[:, (2 * p) * tq:(2 * p + 1) * tq],
                            res[:, (2 * p + 1) * tq:(2 * p + 2) * tq]], axis=0).T


def _ada_kernel(c_ref, w_ref, b_ref, o_ref):
    c = c_ref[...]
    cond = c * (1.0 / (1.0 + jnp.exp(-c)))
    o_ref[0] = jnp.dot(cond, w_ref[0], precision=lax.Precision.HIGHEST,
                       preferred_element_type=F32) + b_ref[0]


def _ada_call(cc, w_ada, b_ada):
    n_layers, d, n_out = w_ada.shape
    rows = cc.shape[0]
    return pl.pallas_call(
        _ada_kernel,
        grid=(n_layers, n_out // ADA_TN),
        in_specs=[
            pl.BlockSpec((rows, d), lambda l, n: (0, 0)),
            pl.BlockSpec((1, d, ADA_TN), lambda l, n: (l, 0, n)),
            pl.BlockSpec((1, 1, ADA_TN), lambda l, n: (l, 0, n)),
        ],
        out_specs=pl.BlockSpec((1, rows, ADA_TN), lambda l, n: (l, 0, n)),
        out_shape=jax.ShapeDtypeStruct((n_layers, rows, n_out), F32),
        compiler_params=_cparams(),
        name="adaln_mod",
    )(cc, w_ada, b_ada.reshape(n_layers, 1, n_out))


def _inproj_kernel(*refs, block_plan, rope):
    if rope:
        x_ref, mod_ref, wqk_ref, wvt_ref, g_ref, cos_ref, sa_ref, sb_ref, oqk_ref, ovt_ref = refs
    else:
        x_ref, mod_ref, wqk_ref, wvt_ref, g_ref, oqk_ref, ovt_ref = refs
    mod = mod_ref[0]
    h = (x_ref[0] * (1.0 + mod[1:2]) + mod[0:1]).astype(BF16)
    qk = jnp.dot(h, wqk_ref[...], preferred_element_type=F32)
    vt = lax.dot_general(wvt_ref[...], h, (((1,), (1,)), ((), ())), preferred_element_type=F32)
    ovt_ref[0] = vt.astype(BF16)
    gm = _group_sum_matrix() if any(p[0] is not None for p in block_plan) else None
    if rope:
        cos, sa, sb = cos_ref[...], sa_ref[...], sb_ref[...]
    for j, (gain_row, do_rope, scale) in enumerate(block_plan):
        blk = qk[:, j * LANES:(j + 1) * LANES]
        if gain_row is not None:
            blk = _head_rms(blk, gm, g_ref[gain_row:gain_row + 1, :])
        if do_rope and rope:
            blk = _rope(blk, cos, sa, sb)
        if scale != 1.0:
            blk = blk * scale
        oqk_ref[0, :, j * LANES:(j + 1) * LANES] = blk.astype(BF16)


def _inproj_call(x, mod, mod_row, wqk, wvt, gains, block_plan, rope_tabs, tm, name):
    b, t, d = x.shape
    wq = wqk.shape[1]
    wv = wvt.shape[0]
    rope = rope_tabs is not None
    mod_map = (lambda bi, i: (bi, 0, 0)) if mod_row is None else (lambda bi, i: (mod_row, 0, 0))
    in_specs = [
        pl.BlockSpec((1, tm, d), lambda bi, i: (bi, i, 0)),
        pl.BlockSpec((1,) + mod.shape[1:], mod_map),
        _const_spec(wqk.shape),
        _const_spec(wvt.shape),
        _const_spec(gains.shape),
    ]
    args = [x, mod, wqk, wvt, gains]
    if rope:
        in_specs += [pl.BlockSpec((tm, LANES), lambda bi, i: (i, 0))] * 3
        args += list(rope_tabs)
    return pl.pallas_call(
        functools.partial(_inproj_kernel, block_plan=block_plan, rope=rope),
        grid=(b, t // tm),
        in_specs=in_specs,
        out_specs=[
            pl.BlockSpec((1, tm, wq), lambda bi, i: (bi, i, 0)),
            pl.BlockSpec((1, wv, tm), lambda bi, i: (bi, 0, i)),
        ],
        out_shape=[jax.ShapeDtypeStruct((b, t, wq), BF16), jax.ShapeDtypeStruct((b, wv, t), BF16)],
        compiler_params=_cparams(),
        name=name,
    )(*args)


def _gqa_kernel(*refs, tq, latent_keys):
    if latent_keys:
        q_ref, kc_ref, vc_ref, kl_ref, vl_ref, o_ref = refs
    else:
        q_ref, kc_ref, vc_ref, o_ref = refs
        kl_ref = vl_ref = None
    q = q_ref[0]
    parts = _split_heads_rows(q[:, :LANES]) + _split_heads_rows(q[:, LANES:])
    (res,) = _attend([(jnp.concatenate(parts, axis=0), _key_chunks(kc_ref, vc_ref, kl_ref, vl_ref))])
    o_ref[0] = jnp.concatenate([_stack_head_rows(res, tq, 0), _stack_head_rows(res, tq, 1)],
                               axis=1).astype(BF16)


def _gqa_call(qk_q, qk_ctx, vt_ctx, qk_lat, vt_lat, tq, name):
    b, t, _ = qk_q.shape
    n_ctx = qk_ctx.shape[1]
    k_blk0 = (GQA_KV_HEADS * 4 * HEAD_DIM) // LANES
    latent_keys = qk_lat is not None
    in_specs = [
        pl.BlockSpec((1, tq, 2 * LANES), lambda bi, g, i: (bi, i, g)),
        pl.BlockSpec((1, n_ctx, LANES), lambda bi, g, i: (bi, 0, k_blk0 + g)),
        pl.BlockSpec((1, HEAD_DIM, n_ctx), lambda bi, g, i: (bi, g, 0)),
    ]
    args = [qk_q, qk_ctx, vt_ctx]
    if latent_keys:
        n_lat = qk_lat.shape[1]
        in_specs += [
            pl.BlockSpec((1, n_lat, LANES), lambda bi, g, i: (bi, 0, k_blk0 + g)),
            pl.BlockSpec((1, HEAD_DIM, n_lat), lambda bi, g, i: (bi, g, 0)),
        ]
        args += [qk_lat, vt_lat]
    return pl.pallas_call(
        functools.partial(_gqa_kernel, tq=tq, latent_keys=latent_keys),
        grid=(b, GQA_KV_HEADS, t // tq),
        in_specs=in_specs,
        out_specs=pl.BlockSpec((1, tq, 2 * LANES), lambda bi, g, i: (bi, i, g)),
        out_shape=jax.ShapeDtypeStruct((b, t, GQA_KV_HEADS * 2 * LANES), BF16),
        compiler_params=_cparams(),
        name=name,
    )(*args)


def _diff_kernel(*refs, tq, latent_keys, lambda_init):
    if latent_keys:
        q_ref, kc_ref, vc_ref, kl_ref, vl_ref, lam_ref, g_ref, o_ref = refs
    else:
        q_ref, kc_ref, vc_ref, lam_ref, g_ref, o_ref = refs
        kl_ref = vl_ref = None
    q_stack = jnp.concatenate(_split_heads_rows(q_ref[0]), axis=0)
    (res,) = _attend([(q_stack, _key_chunks(kc_ref, vc_ref, kl_ref, vl_ref))])
    lv = lam_ref[...]
    lam = (jnp.exp(jnp.sum(lv[0:1] * lv[1:2], axis=1, keepdims=True))
           - jnp.exp(jnp.sum(lv[2:3] * lv[3:4], axis=1, keepdims=True)) + lambda_init)
    o = (res[:, :tq] - lam * res[:, tq:]).T
    o = o * lax.rsqrt(jnp.mean(o * o, axis=-1, keepdims=True) + EPS) * g_ref[...]
    o_ref[0] = (o * (1.0 - lambda_init)).astype(BF16)


def _diff_call(qk_q, qk_ctx, vt_ctx, qk_lat, vt_lat, lam_vec, subln_g, lambda_init, tq, name):
    b, t, wqk = qk_q.shape
    n_ctx = qk_ctx.shape[1]
    n_heads = wqk // (4 * LANES)
    q_blk0, k_blk0, v_blk0 = 2 * n_heads, 3 * n_heads, n_heads
    latent_keys = qk_lat is not None
    in_specs = [
        pl.BlockSpec((1, tq, LANES), lambda bi, h, i: (bi, i, q_blk0 + h)),
        pl.BlockSpec((1, n_ctx, LANES), lambda bi, h, i: (bi, 0, k_blk0 + h)),
        pl.BlockSpec((1, LANES, n_ctx), lambda bi, h, i: (bi, v_blk0 + h, 0)),
    ]
    args = [qk_q, qk_ctx, vt_ctx]
    if latent_keys:
        n_lat = qk_lat.shape[1]
        in_specs += [
            pl.BlockSpec((1, n_lat, LANES), lambda bi, h, i: (bi, 0, k_blk0 + h)),
            pl.BlockSpec((1, LANES, n_lat), lambda bi, h, i: (bi, v_blk0 + h, 0)),
        ]
        args += [qk_lat, vt_lat]
    in_specs += [_const_spec(lam_vec.shape), _const_spec((1, LANES))]
    args += [lam_vec, subln_g.reshape(1, LANES)]
    return pl.pallas_call(
        functools.partial(_diff_kernel, tq=tq, latent_keys=latent_keys, lambda_init=lambda_init),
        grid=(b, n_heads, t // tq),
        in_specs=in_specs,
        out_specs=pl.BlockSpec((1, tq, LANES), lambda bi, h, i: (bi, i, h)),
        out_shape=jax.ShapeDtypeStruct((b, t, n_heads * LANES), BF16),
        compiler_params=_cparams(),
        name=name,
    )(*args)


def _na_kernel(*refs, n_local):
    q_ref, kc_ref, vc_ref = refs[:3]
    k_refs = refs[3:3 + n_local]
    v_refs = refs[3 + n_local:3 + 2 * n_local]
    bias_ref = refs[3 + 2 * n_local] if n_local else None
    o_ref = refs[-1]
    tq = q_ref.shape[1]
    n_pairs = q_ref.shape[2] // LANES
    streams = []
    for p in range(n_pairs):
        cols = slice(p * LANES, (p + 1) * LANES)
        q_stack = jnp.concatenate(_split_heads_rows(q_ref[0, :, cols]), axis=0)
        chunks = [(kc_ref[0, :, cols], vc_ref[0, cols, :], None)]
        for c in range(n_local):
            chunks.append((k_refs[c][0, :, cols], v_refs[c][0, cols, :], bias_ref[0, p, c * tq:(c + 1) * tq, :]))
        streams.append((q_stack, chunks))
    for p, res in enumerate(_attend(streams)):
        o_ref[0, :, p * LANES:(p + 1) * LANES] = _pick_head_rows(res, tq, 0).astype(BF16)


def _na_call(qk_lat, vt_lat, qk_ctx, vt_ctx, bias_tab, name):
    b, s, _ = qk_lat.shape
    n_ctx = qk_ctx.shape[1]
    tq = NA_ROWS_PER_BLOCK * GRID_W
    nj = s // tq
    n_pairs = bias_tab.shape[1]
    w = n_pairs * LANES
    n_local = 3

    def tb(j):
        return jnp.clip(j - 1, 0, nj - n_local)

    def cls(j):
        return jnp.where(j == 0, 0, jnp.where(j == nj - 1, 2, 1))

    k_specs = [pl.BlockSpec((1, tq, w), functools.partial(lambda j, bi, d: (bi, tb(j) + d, 1), d=d))
               for d in range(n_local)]
    v_specs = [pl.BlockSpec((1, w, tq), functools.partial(lambda j, bi, d: (bi, 0, tb(j) + d), d=d))
               for d in range(n_local)]
    return pl.pallas_call(
        functools.partial(_na_kernel, n_local=n_local),
        grid=(nj, b),
        in_specs=[
            pl.BlockSpec((1, tq, w), lambda j, bi: (bi, j, 0)),
            pl.BlockSpec((1, n_ctx, w), lambda j, bi: (bi, 0, 1)),
            pl.BlockSpec((1, w, n_ctx), lambda j, bi: (bi, 0, 0)),
            *k_specs, *v_specs,
            pl.BlockSpec((1, n_pairs, n_local * tq, 2 * tq), lambda j, bi: (cls(j), 0, 0, 0)),
        ],
        out_specs=pl.BlockSpec((1, tq, w), lambda j, bi: (bi, j, 0)),
        out_shape=jax.ShapeDtypeStruct((b, s, w), BF16),
        compiler_params=_cparams(),
        name=name,
    )(qk_lat, qk_ctx, vt_ctx, *([qk_lat] * n_local), *([vt_lat] * n_local), bias_tab)


def _na_ctx_call(qk_ctx, vt_ctx, n_pairs, name):
    b, n_ctx, _ = qk_ctx.shape
    w = n_pairs * LANES
    return pl.pallas_call(
        functools.partial(_na_kernel, n_local=0),
        grid=(b,),
        in_specs=[
            pl.BlockSpec((1, n_ctx, w), lambda bi: (bi, 0, 0)),
            pl.BlockSpec((1, n_ctx, w), lambda bi: (bi, 0, 1)),
            pl.BlockSpec((1, w, n_ctx), lambda bi: (bi, 0, 0)),
        ],
        out_specs=pl.BlockSpec((1, n_ctx, w), lambda bi: (bi, 0, 0)),
        out_shape=jax.ShapeDtypeStruct((b, n_ctx, w), BF16),
        compiler_params=_cparams(),
        name=name,
    )(qk_ctx, qk_ctx, vt_ctx)


def _na_bias_tables(rpb, seq):
    n_heads = rpb.shape[0]
    rows = seq // GRID_W
    tq = NA_ROWS_PER_BLOCK * GRID_W
    nj = seq // tq
    kh = min(NA_KH, rows)
    n_dc = rpb.shape[2]
    pad = GRID_W - 1 - (n_dc - 1) // 2
    rpb_pad = jnp.pad(rpb.astype(F32), ((0, 0), (0, 0), (pad, pad)), mode="edge")
    col_tab = jnp.stack([rpb_pad[:, :, GRID_W - 1 - qc:2 * GRID_W - 1 - qc] for qc in range(GRID_W)], axis=-1)
    tabs = []
    for j in (0, 1, nj - 1):
        ws = NA_ROWS_PER_BLOCK * int(np.clip(j - 1, 0, nj - 3))
        n_kr = 3 * NA_ROWS_PER_BLOCK
        kr = ws + np.arange(n_kr)
        r = NA_ROWS_PER_BLOCK * j + np.arange(NA_ROWS_PER_BLOCK)
        dr = np.clip(kr[:, None] - r[None, :] + NA_KH - 1, 0, 2 * NA_KH - 2)
        blocks = jnp.stack([jnp.stack([col_tab[:, dr[a, q]] for q in range(NA_ROWS_PER_BLOCK)], axis=1)
                            for a in range(n_kr)], axis=1)
        vals = blocks.transpose(0, 1, 3, 2, 4).reshape(n_heads, 3 * tq, tq)
        kk = np.arange(3 * tq)
        kr_f, kc = ws + kk // GRID_W, kk % GRID_W
        qq = np.arange(tq)
        r_f, cq = NA_ROWS_PER_BLOCK * j + qq // GRID_W, qq % GRID_W
        rs = np.clip(r_f - kh // 2, 0, rows - kh)
        cs = np.clip(cq - NA_KW // 2, 0, GRID_W - NA_KW)
        ok = ((kr_f[:, None] >= rs[None, :]) & (kr_f[:, None] < rs[None, :] + kh)
              & (kc[:, None] >= cs[None, :]) & (kc[:, None] < cs[None, :] + NA_KW))
        tabs.append(jnp.where(ok[None], vals, -jnp.inf))
    tab = jnp.stack(tabs)
    tab = tab.reshape(3, n_heads // 2, 2, 3 * tq, tq).transpose(0, 1, 3, 2, 4)
    return tab.reshape(3, n_heads // 2, 3 * tq, 2 * tq)


def _outproj_kernel(*refs, n_att):
    att_refs = refs[:n_att]
    x_ref, mod_ref, w_ref, lng_ref, lnb_ref, o_ref = refs[n_att:]
    y = None
    k0 = 0
    for a_ref in att_refs:
        kw = a_ref.shape[2]
        part = jnp.dot(a_ref[0], w_ref[k0:k0 + kw, :], preferred_element_type=F32)
        y = part if y is None else y + part
        k0 += kw
    z = ALPHA * x_ref[0] + mod_ref[0][2:3] * y
    o_ref[0] = _layer_norm(z, lng_ref[...], lnb_ref[...])


def _outproj_call(atts, x, mod, mod_row, w_o, ln_g, ln_b, tm, name):
    b, t, d = x.shape
    mod_map = (lambda bi, i: (bi, 0, 0)) if mod_row is None else (lambda bi, i: (mod_row, 0, 0))
    in_specs = [pl.BlockSpec((1, tm, a.shape[2]), lambda bi, i: (bi, i, 0)) for a in atts]
    in_specs += [
        pl.BlockSpec((1, tm, d), lambda bi, i: (bi, i, 0)),
        pl.BlockSpec((1,) + mod.shape[1:], mod_map),
        _const_spec(w_o.shape),
        _const_spec((1, d)),
        _const_spec((1, d)),
    ]
    return pl.pallas_call(
        functools.partial(_outproj_kernel, n_att=len(atts)),
        grid=(b, t // tm),
        in_specs=in_specs,
        out_specs=pl.BlockSpec((1, tm, d), lambda bi, i: (bi, i, 0)),
        out_shape=jax.ShapeDtypeStruct((b, t, d), F32),
        compiler_params=_cparams(),
        name=name,
    )(*atts, x, mod, w_o, ln_g.reshape(1, d), ln_b.reshape(1, d))


def _ffn_kernel(*refs, tm, halo):
    if halo:
        (x_ref, xp_ref, xn_ref, mod_ref, wup_ref, cw_ref, cb_ref, wdn_ref, lng_ref, lnb_ref,
         o_ref, hs_ref, act_ref) = refs
    else:
        x_ref, mod_ref, wup_ref, cw_ref, cb_ref, wdn_ref, lng_ref, lnb_ref, o_ref, hs_ref, act_ref = refs
    d = x_ref.shape[2]
    d_ff = wdn_ref.shape[0]
    rows = tm + HALO
    mod = mod_ref[0]
    shift, scale, gate = mod[3:4], mod[4:5], mod[5:6]
    x = x_ref[0]
    hs_ref[0:tm, :] = (x * (1.0 + scale) + shift).astype(BF16)
    zeros8 = jnp.zeros((SUBLANES_F32, d), F32)
    if halo:
        i = pl.program_id(1)
        last = pl.num_programs(1) - 1
        hp = jnp.where(i > 0, xp_ref[0] * (1.0 + scale) + shift, 0.0)
        hn = jnp.where(i < last, xn_ref[0] * (1.0 + scale) + shift, 0.0)
    else:
        hp = hn = zeros8
    hs_ref[tm:rows, :] = jnp.concatenate([hn, hp], axis=0).astype(BF16)
    hs = hs_ref[...]
    n_chunks = d_ff // FF_CHUNK

    def up_proj(cols):
        return jnp.dot(hs, wup_ref[:, cols], preferred_element_type=F32)

    def conv(up, cols):
        w = cw_ref[:, cols]
        return (pltpu.roll(up, 1, 0)[0:tm] * w[0:1] + up[0:tm] * w[1:2]
                + pltpu.roll(up, rows - 1, 0)[0:tm] * w[2:3] + cb_ref[:, cols])

    ups = {}
    for t in range(n_chunks + 1):
        if t < n_chunks:
            a_cols = slice(t * FF_CHUNK, (t + 1) * FF_CHUNK)
            g_cols = slice(d_ff + t * FF_CHUNK, d_ff + (t + 1) * FF_CHUNK)
            ups[t] = (up_proj(a_cols), up_proj(g_cols), a_cols, g_cols)
        if t >= 1:
            up_a, up_g, a_cols, g_cols = ups.pop(t - 1)
            act_ref[:, a_cols] = (_gelu_tanh(conv(up_g, g_cols)) * conv(up_a, a_cols)).astype(BF16)
    acc = jnp.dot(act_ref[...], wdn_ref[...], preferred_element_type=F32)
    z = ALPHA * x + gate * acc
    o_ref[0] = _layer_norm(z, lng_ref[...], lnb_ref[...])


def _ffn_call(x, mod, mod_row, w_up, conv_w, conv_b, w_down, ln_g, ln_b, tm, name):
    b, t, d = x.shape
    halo = t > tm
    mod_map = (lambda bi, i: (bi, 0, 0)) if mod_row is None else (lambda bi, i: (mod_row, 0, 0))
    in_specs = [pl.BlockSpec((1, tm, d), lambda bi, i: (bi, i, 0))]
    args = [x]
    if halo:
        r8 = tm // SUBLANES_F32
        n8 = t // SUBLANES_F32
        in_specs += [
            pl.BlockSpec((1, SUBLANES_F32, d), lambda bi, i: (bi, jnp.maximum(i * r8 - 1, 0), 0)),
            pl.BlockSpec((1, SUBLANES_F32, d), lambda bi, i: (bi, jnp.minimum((i + 1) * r8, n8 - 1), 0)),
        ]
        args += [x, x]
    in_specs += [
        pl.BlockSpec((1,) + mod.shape[1:], mod_map),
        _const_spec(w_up.shape),
        _const_spec(conv_w.shape),
        _const_spec(conv_b.shape),
        _const_spec(w_down.shape),
        _const_spec((1, d)),
        _const_spec((1, d)),
    ]
    args += [mod, w_up, conv_w, conv_b, w_down, ln_g.reshape(1, d), ln_b.reshape(1, d)]
    return pl.pallas_call(
        functools.partial(_ffn_kernel, tm=tm, halo=halo),
        grid=(b, t // tm),
        in_specs=in_specs,
        out_specs=pl.BlockSpec((1, tm, d), lambda bi, i: (bi, i, 0)),
        out_shape=jax.ShapeDtypeStruct((b, t, d), F32),
        scratch_shapes=[pltpu.VMEM((tm + HALO, d), BF16), pltpu.VMEM((tm, w_down.shape[0]), BF16)],
        compiler_params=_cparams(),
        name=name,
    )(*args)


def _rope_tables(seq):
    n_freq = HEAD_DIM // 4
    inv = ROPE_BASE ** (-jnp.arange(n_freq, dtype=F32) / n_freq)
    t = jnp.arange(seq)
    row = (t // GRID_W).astype(F32)
    col = (t % GRID_W).astype(F32)
    ar = row[:, None] * inv
    ac = col[:, None] * inv
    ang = jnp.concatenate([ar, ar, ac, ac], -1)
    cos, sin = jnp.cos(ang), jnp.sin(ang)
    even_chunk = ((np.arange(HEAD_DIM) // n_freq) % 2 == 0)[None, :]
    sin_a = jnp.where(even_chunk, -sin, 0.0)
    sin_b = jnp.where(even_chunk, 0.0, sin)
    return tuple(jnp.concatenate([a, a], axis=1) for a in (cos, sin_a, sin_b))


def _dup_heads(w, n_heads):
    d = w.shape[0]
    w = w.reshape(d, n_heads, HEAD_DIM)
    return jnp.concatenate([w, w], axis=-1).reshape(d, n_heads * PAIR)


def kernel(x, c, ctx, c_ctx, w_ada, b_ada, ln_g, ln_b, w_in_ab, w_o_ab, na_rpb, diff_lambda, diff_subln,
           w_in_c, w_o_c, gqa_qk_norm, w_up, conv_w, conv_b, w_down):
    b, s, d = x.shape
    n_ctx = ctx.shape[1]
    d_ff = w_down.shape[1]
    na_w = na_rpb.shape[1] * HEAD_DIM
    diff_w = d - na_w
    gqa_kw = GQA_KV_HEADS * HEAD_DIM
    tm_lat = min(TM_LATENT, s)

    mod_rows = -(-(b + 1) // SUBLANES_F32) * SUBLANES_F32
    ctx_row = b
    cc = jnp.concatenate([c, c_ctx[None, :], jnp.zeros((mod_rows - b - 1, d), F32)], axis=0)
    mod_all = _ada_call(cc, w_ada, b_ada)
    rope_tabs = _rope_tables(s)
    ones_gain = jnp.ones((2, LANES), F32)

    xc = ctx
    for l in range(DEPTH):
        need_ctx = l < DEPTH - 1
        i = l // 2
        mod = mod_all[l].reshape(mod_rows, 6, d)
        if l % 2 == 0:
            lambda_init = 0.8 - 0.6 * math.exp(-0.3 * l)
            w = w_in_ab[i]
            o0 = 3 * na_w
            wqk = jnp.concatenate([w[:, :2 * na_w], w[:, o0:o0 + 2 * diff_w]], axis=1).astype(BF16)
            wvt = jnp.concatenate([w[:, 2 * na_w:o0], w[:, o0 + 2 * diff_w:]], axis=1).T.astype(BF16)
            nb = na_w // LANES
            plan = ([(None, False, Q_SCALE)] * nb + [(None, False, 1.0)] * nb
                    + [(None, True, Q_SCALE)] * (diff_w // LANES) + [(None, True, 1.0)] * (diff_w // LANES))
            gains = ones_gain
        else:
            w = w_in_c[i]
            q_w = d
            wqk = jnp.concatenate([w[:, :q_w], _dup_heads(w[:, q_w:q_w + gqa_kw], GQA_KV_HEADS)],
                                  axis=1).astype(BF16)
            wvt = w[:, q_w + gqa_kw:].T.astype(BF16)
            plan = [(0, True, Q_SCALE)] * (q_w // LANES) + [(1, True, 1.0)] * GQA_KV_HEADS
            gains = jnp.concatenate([gqa_qk_norm[i], gqa_qk_norm[i]], axis=1)
        plan = tuple(plan)

        qk_lat, vt_lat = _inproj_call(x, mod, None, wqk, wvt, gains, plan, rope_tabs, tm_lat, f"inproj_lat_{l}")
        qk_ctx, vt_ctx = _inproj_call(xc, mod, ctx_row, wqk, wvt, gains, plan, None, n_ctx, f"inproj_ctx_{l}")

        if l % 2 == 0:
            bias_tab = _na_bias_tables(na_rpb[i], s) * LOG2E
            atts = [
                _na_call(qk_lat, vt_lat, qk_ctx, vt_ctx, bias_tab, f"na_lat_{l}"),
                _diff_call(qk_lat, qk_ctx, vt_ctx, qk_lat, vt_lat, diff_lambda[i], diff_subln[i], lambda_init,
                           min(TQ_DIFF, s), f"diff_lat_{l}"),
            ]
            w_o = w_o_ab[i].astype(BF16)
            if need_ctx:
                atts_c = [
                    _na_ctx_call(qk_ctx, vt_ctx, na_w // LANES, f"na_ctx_{l}"),
                    _diff_call(qk_ctx, qk_ctx, vt_ctx, None, None, diff_lambda[i], diff_subln[i], lambda_init,
                               n_ctx, f"diff_ctx_{l}"),
                ]
        else:
            atts = [_gqa_call(qk_lat, qk_ctx, vt_ctx, qk_lat, vt_lat, min(TQ_GQA, s), f"gqa_lat_{l}")]
            w_o = w_o_c[i].astype(BF16)
            if need_ctx:
                atts_c = [_gqa_call(qk_ctx, qk_ctx, vt_ctx, None, None, n_ctx, f"gqa_ctx_{l}")]

        w_up_l = w_up[l].astype(BF16)
        conv_w_l = conv_w[l]
        conv_b_l = conv_b[l].reshape(1, 2 * d_ff)
        w_down_l = w_down[l].astype(BF16)

        x = _outproj_call(atts, x, mod, None, w_o, ln_g[l, 0], ln_b[l, 0], tm_lat, f"outproj_lat_{l}")
        x = _ffn_call(x, mod, None, w_up_l, conv_w_l, conv_b_l, w_down_l, ln_g[l, 1], ln_b[l, 1],
                      tm_lat, f"ffn_lat_{l}")
        if need_ctx:
            xc = _outproj_call(atts_c, xc, mod, ctx_row, w_o, ln_g[l, 0], ln_b[l, 0], n_ctx, f"outproj_ctx_{l}")
            xc = _ffn_call(xc, mod, ctx_row, w_up_l, conv_w_l, conv_b_l, w_down_l, ln_g[l, 1], ln_b[l, 1],
                           n_ctx, f"ffn_ctx_{l}")
    return x
```

```python
import functools
import math

import numpy as np
import jax
import jax.numpy as jnp
from jax import lax
from jax.experimental import pallas as pl
from jax.experimental.pallas import tpu as pltpu

F32 = jnp.float32
BF16 = jnp.bfloat16

DEPTH = 4
GRID_W = 64
HEAD_DIM = 64
NA_KH = 8
NA_KW = 16
GQA_KV_HEADS = 4
CONV_WIDTH = 3
ROPE_BASE = 10000.0
EPS = 1e-6
ALPHA = (2 * DEPTH) ** 0.25
LOG2E = math.log2(math.e)
Q_SCALE = HEAD_DIM ** -0.5 * LOG2E

LANES = 128
SUBLANES_F32 = 8
SUBLANES_BF16 = 16
MXU_DIM = 256
VMEM_LIMIT_BYTES = 56 * 1024 * 1024

PAIR = 2 * HEAD_DIM
assert PAIR == LANES

TM_LATENT = 512
TQ_GQA = 512
TQ_DIFF = 1024
NA_ROWS_PER_BLOCK = 4
KEY_CHUNK = MXU_DIM
ONES_ROWS = SUBLANES_BF16
FF_CHUNK = MXU_DIM
ADA_TN = 1536
INPROJ_GROUP = MXU_DIM
HALO = SUBLANES_BF16


def _cparams():
    return pltpu.CompilerParams(vmem_limit_bytes=VMEM_LIMIT_BYTES)


def _const_spec(shape):
    nd = len(shape)
    return pl.BlockSpec(shape, lambda *_: (0,) * nd, pipeline_mode=pl.Buffered(1))


def _layer_norm(z, g, b):
    mu = jnp.mean(z, axis=-1, keepdims=True)
    d = z - mu
    var = jnp.mean(d * d, axis=-1, keepdims=True)
    return d * lax.rsqrt(var + EPS) * g + b


def _gelu_tanh(x):
    c = math.sqrt(2.0 / math.pi)
    return (0.5 * x) * (1.0 + jnp.tanh(x * (c + (0.044715 * c) * (x * x))))


def _group_sum_matrix():
    r = lax.broadcasted_iota(jnp.int32, (2 * LANES, LANES), 0)
    c = lax.broadcasted_iota(jnp.int32, (2 * LANES, LANES), 1)
    same = ((r % LANES) // HEAD_DIM) == (c // HEAD_DIM)
    return jnp.where(same, 1.0, 0.0).astype(BF16)


def _head_rms(xb, gm, g):
    x2 = xb * xb
    hi = x2.astype(BF16)
    lo = (x2 - hi.astype(F32)).astype(BF16)
    ssq = jnp.dot(jnp.concatenate([hi, lo], axis=1), gm, preferred_element_type=F32)
    return xb * lax.rsqrt(ssq * (1.0 / HEAD_DIM) + EPS) * g


def _rope(xb, cos, sin_a, sin_b):
    return (xb * cos
            + pltpu.roll(xb, LANES - HEAD_DIM // 4, 1) * sin_a
            + pltpu.roll(xb, HEAD_DIM // 4, 1) * sin_b)


def _split_heads_rows(q_pair):
    qf = q_pair.astype(F32)
    lane = lax.broadcasted_iota(jnp.int32, qf.shape, 1)
    lo = lane < HEAD_DIM
    return [jnp.where(lo, qf, 0.0).astype(BF16), jnp.where(lo, 0.0, qf).astype(BF16)]


def _attend(streams):
    n_chunks = len(streams[0][1])
    s_buf, e_buf = {}, {}
    m = [None] * len(streams)
    acc = [None] * len(streams)
    for t in range(n_chunks + 2):
        for i, (q_stack, chunks) in enumerate(streams):
            if t < n_chunks:
                k_c, _, bias_c = chunks[t]
                s = lax.dot_general(k_c, q_stack, (((1,), (1,)), ((), ())), preferred_element_type=F32)
                s_buf[i, t] = s if bias_c is None else s + bias_c
        for i, (q_stack, chunks) in enumerate(streams):
            if 0 <= t - 1 < n_chunks:
                s = s_buf.pop((i, t - 1))
                mc = jnp.max(s, axis=0, keepdims=True)
                m_new = mc if m[i] is None else jnp.maximum(m[i], mc)
                alpha = None if m[i] is None else jnp.exp2(m[i] - m_new)
                m[i] = m_new
                e_buf[i, t - 1] = (jnp.exp2(s - m_new).astype(BF16), alpha)
        for i, (q_stack, chunks) in enumerate(streams):
            if 0 <= t - 2 < n_chunks:
                e, alpha = e_buf.pop((i, t - 2))
                vt_c = chunks[t - 2][1]
                vt_aug = jnp.concatenate([vt_c, jnp.ones((ONES_ROWS, vt_c.shape[1]), BF16)], axis=0)
                pv = jnp.dot(vt_aug, e, preferred_element_type=F32)
                acc[i] = pv if alpha is None else alpha * acc[i] + pv
    outs = []
    for (q_stack, chunks), a in zip(streams, acc):
        dv = chunks[0][1].shape[0]
        outs.append(a[:dv] * (1.0 / a[dv:dv + 1]))
    return outs


def _key_chunks(kc_ref, vc_ref, kl_ref, vl_ref, k_cols=slice(None), v_rows=slice(None)):
    chunks = [(kc_ref[0, :, k_cols], vc_ref[0, v_rows, :], None)]
    if kl_ref is not None:
        n_lat = kl_ref.shape[1]
        for c0 in range(0, n_lat, KEY_CHUNK):
            chunks.append((kl_ref[0, c0:c0 + KEY_CHUNK, k_cols], vl_ref[0, v_rows, c0:c0 + KEY_CHUNK], None))
    return chunks


def _pick_head_rows(res, tq, p):
    row = lax.broadcasted_iota(jnp.int32, (LANES, tq), 0)
    a = res[:, (2 * p) * tq:(2 * p + 1) * tq]
    b = res[:, (2 * p + 1) * tq:(2 * p + 2) * tq]
    return jnp.where(row < HEAD_DIM, a, b).T


def _stack_head_rows(res, tq, p):
    return jnp.concatenate([res[:, (2 * p) * tq:(2 * p + 1) * tq],
                            res[:, (2 * p + 1) * tq:(2 * p + 2) * tq]], axis=0).T


def _ada_kernel(c_ref, w_ref, b_ref, o_ref):
    c = c_ref[...]
    cond = c * (1.0 / (1.0 + jnp.exp(-c)))
    o_ref[0] = jnp.dot(cond, w_ref[0], precision=lax.Precision.HIGHEST,
                       preferred_element_type=F32) + b_ref[0]


def _ada_call(cc, w_ada, b_ada):
    n_layers, d, n_out = w_ada.shape
    rows = cc.shape[0]
    return pl.pallas_call(
        _ada_kernel,
        grid=(n_layers, n_out // ADA_TN),
        in_specs=[
            pl.BlockSpec((rows, d), lambda l, n: (0, 0)),
            pl.BlockSpec((1, d, ADA_TN), lambda l, n: (l, 0, n)),
            pl.BlockSpec((1, 1, ADA_TN), lambda l, n: (l, 0, n)),
        ],
        out_specs=pl.BlockSpec((1, rows, ADA_TN), lambda l, n: (l, 0, n)),
        out_shape=jax.ShapeDtypeStruct((n_layers, rows, n_out), F32),
        compiler_params=_cparams(),
        name="adaln_mod",
    )(cc, w_ada, b_ada.reshape(n_layers, 1, n_out))


def _inproj_kernel(*refs, block_plan, rope):
    if rope:
        x_ref, mod_ref, wqk_ref, wvt_ref, g_ref, cos_ref, sa_ref, sb_ref, oqk_ref, ovt_ref = refs
    else:
        x_ref, mod_ref, wqk_ref, wvt_ref, g_ref, oqk_ref, ovt_ref = refs
    mod = mod_ref[0]
    h = (x_ref[0] * (1.0 + mod[1:2]) + mod[0:1]).astype(BF16)
    has_norm = any(p[0] is not None for p in block_plan)
    gm = _group_sum_matrix() if has_norm else None
    if rope:
        cos, sa, sb = cos_ref[...], sa_ref[...], sb_ref[...]

    def v_proj():
        vt = lax.dot_general(wvt_ref[...], h, (((1,), (1,)), ((), ())), preferred_element_type=F32)
        ovt_ref[0] = vt.astype(BF16)

    def epilogue(j0, qk):
        for jj in range(qk.shape[1] // LANES):
            gain_row, do_rope, scale = block_plan[j0 + jj]
            blk = qk[:, jj * LANES:(jj + 1) * LANES]
            if gain_row is not None:
                blk = _head_rms(blk, gm, g_ref[gain_row:gain_row + 1, :])
            if do_rope and rope:
                blk = _rope(blk, cos, sa, sb)
            if scale != 1.0:
                blk = blk * scale
            oqk_ref[0, :, (j0 + jj) * LANES:(j0 + jj + 1) * LANES] = blk.astype(BF16)

    if not has_norm:
        v_proj()
        epilogue(0, jnp.dot(h, wqk_ref[...], preferred_element_type=F32))
        return
    bpg = INPROJ_GROUP // LANES
    n_groups = len(block_plan) // bpg
    prev = None
    for g in range(n_groups + 1):
        cur = None
        if g < n_groups:
            cur = jnp.dot(h, wqk_ref[:, g * INPROJ_GROUP:(g + 1) * INPROJ_GROUP], preferred_element_type=F32)
        else:
            v_proj()
        if prev is not None:
            epilogue((g - 1) * bpg, prev)
        prev = cur


def _inproj_call(x, mod, mod_row, wqk, wvt, gains, block_plan, rope_tabs, tm, name):
    b, t, d = x.shape
    wq = wqk.shape[1]
    wv = wvt.shape[0]
    rope = rope_tabs is not None
    mod_map = (lambda bi, i: (bi, 0, 0)) if mod_row is None else (lambda bi, i: (mod_row, 0, 0))
    in_specs = [
        pl.BlockSpec((1, tm, d), lambda bi, i: (bi, i, 0)),
        pl.BlockSpec((1,) + mod.shape[1:], mod_map),
        _const_spec(wqk.shape),
        _const_spec(wvt.shape),
        _const_spec(gains.shape),
    ]
    args = [x, mod, wqk, wvt, gains]
    if rope:
        in_specs += [pl.BlockSpec((tm, LANES), lambda bi, i: (i, 0))] * 3
        args += list(rope_tabs)
    return pl.pallas_call(
        functools.partial(_inproj_kernel, block_plan=block_plan, rope=rope),
        grid=(b, t // tm),
        in_specs=in_specs,
        out_specs=[
            pl.BlockSpec((1, tm, wq), lambda bi, i: (bi, i, 0)),
            pl.BlockSpec((1, wv, tm), lambda bi, i: (bi, 0, i)),
        ],
        out_shape=[jax.ShapeDtypeStruct((b, t, wq), BF16), jax.ShapeDtypeStruct((b, wv, t), BF16)],
        compiler_params=_cparams(),
        name=name,
    )(*args)


def _gqa_kernel(*refs, tq, latent_keys):
    if latent_keys:
        q_ref, kc_ref, vc_ref, kl_ref, vl_ref, o_ref = refs
    else:
        q_ref, kc_ref, vc_ref, o_ref = refs
        kl_ref = vl_ref = None
    n_groups = q_ref.shape[2] // (2 * LANES)
    streams = []
    for g in range(n_groups):
        q = q_ref[0, :, g * 2 * LANES:(g + 1) * 2 * LANES]
        parts = _split_heads_rows(q[:, :LANES]) + _split_heads_rows(q[:, LANES:])
        chunks = _key_chunks(kc_ref, vc_ref, kl_ref, vl_ref, slice(g * LANES, (g + 1) * LANES),
                             slice(g * HEAD_DIM, (g + 1) * HEAD_DIM))
        streams.append((jnp.concatenate(parts, axis=0), chunks))
    for g, res in enumerate(_attend(streams)):
        o_ref[0, :, g * 2 * LANES:(g + 1) * 2 * LANES] = jnp.concatenate(
            [_stack_head_rows(res, tq, 0), _stack_head_rows(res, tq, 1)], axis=1).astype(BF16)


def _gqa_call(qk_q, qk_ctx, vt_ctx, qk_lat, vt_lat, tq, groups, name):
    b, t, _ = qk_q.shape
    n_ctx = qk_ctx.shape[1]
    k_blk0 = (GQA_KV_HEADS * 4 * HEAD_DIM) // (groups * LANES)
    latent_keys = qk_lat is not None
    in_specs = [
        pl.BlockSpec((1, tq, groups * 2 * LANES), lambda bi, g, i: (bi, i, g)),
        pl.BlockSpec((1, n_ctx, groups * LANES), lambda bi, g, i: (bi, 0, k_blk0 + g)),
        pl.BlockSpec((1, groups * HEAD_DIM, n_ctx), lambda bi, g, i: (bi, g, 0)),
    ]
    args = [qk_q, qk_ctx, vt_ctx]
    if latent_keys:
        n_lat = qk_lat.shape[1]
        in_specs += [
            pl.BlockSpec((1, n_lat, groups * LANES), lambda bi, g, i: (bi, 0, k_blk0 + g)),
            pl.BlockSpec((1, groups * HEAD_DIM, n_lat), lambda bi, g, i: (bi, g, 0)),
        ]
        args += [qk_lat, vt_lat]
    return pl.pallas_call(
        functools.partial(_gqa_kernel, tq=tq, latent_keys=latent_keys),
        grid=(b, GQA_KV_HEADS // groups, t // tq),
        in_specs=in_specs,
        out_specs=pl.BlockSpec((1, tq, groups * 2 * LANES), lambda bi, g, i: (bi, i, g)),
        out_shape=jax.ShapeDtypeStruct((b, t, GQA_KV_HEADS * 2 * LANES), BF16),
        compiler_params=_cparams(),
        name=name,
    )(*args)


def _diff_kernel(*refs, tq, latent_keys, lambda_init):
    if latent_keys:
        q_ref, kc_ref, vc_ref, kl_ref, vl_ref, lam_ref, g_ref, o_ref = refs
    else:
        q_ref, kc_ref, vc_ref, lam_ref, g_ref, o_ref = refs
        kl_ref = vl_ref = None
    n_heads = q_ref.shape[2] // LANES
    streams = []
    for hd in range(n_heads):
        cols = slice(hd * LANES, (hd + 1) * LANES)
        q_stack = jnp.concatenate(_split_heads_rows(q_ref[0, :, cols]), axis=0)
        streams.append((q_stack, _key_chunks(kc_ref, vc_ref, kl_ref, vl_ref, cols, cols)))
    lv = lam_ref[...]
    lam = (jnp.exp(jnp.sum(lv[0:1] * lv[1:2], axis=1, keepdims=True))
           - jnp.exp(jnp.sum(lv[2:3] * lv[3:4], axis=1, keepdims=True)) + lambda_init)
    for hd, res in enumerate(_attend(streams)):
        o = (res[:, :tq] - lam * res[:, tq:]).T
        o = o * lax.rsqrt(jnp.mean(o * o, axis=-1, keepdims=True) + EPS) * g_ref[...]
        o_ref[0, :, hd * LANES:(hd + 1) * LANES] = (o * (1.0 - lambda_init)).astype(BF16)


def _diff_call(qk_q, qk_ctx, vt_ctx, qk_lat, vt_lat, lam_vec, subln_g, lambda_init, tq, heads, name):
    b, t, wqk = qk_q.shape
    n_ctx = qk_ctx.shape[1]
    n_heads = wqk // (4 * LANES)
    n_steps = n_heads // heads
    q_blk0, k_blk0, v_blk0 = 2 * n_steps, 3 * n_steps, n_steps
    w = heads * LANES
    latent_keys = qk_lat is not None
    in_specs = [
        pl.BlockSpec((1, tq, w), lambda bi, h, i: (bi, i, q_blk0 + h)),
        pl.BlockSpec((1, n_ctx, w), lambda bi, h, i: (bi, 0, k_blk0 + h)),
        pl.BlockSpec((1, w, n_ctx), lambda bi, h, i: (bi, v_blk0 + h, 0)),
    ]
    args = [qk_q, qk_ctx, vt_ctx]
    if latent_keys:
        n_lat = qk_lat.shape[1]
        in_specs += [
            pl.BlockSpec((1, n_lat, w), lambda bi, h, i: (bi, 0, k_blk0 + h)),
            pl.BlockSpec((1, w, n_lat), lambda bi, h, i: (bi, v_blk0 + h, 0)),
        ]
        args += [qk_lat, vt_lat]
    in_specs += [_const_spec(lam_vec.shape), _const_spec((1, LANES))]
    args += [lam_vec, subln_g.reshape(1, LANES)]
    return pl.pallas_call(
        functools.partial(_diff_kernel, tq=tq, latent_keys=latent_keys, lambda_init=lambda_init),
        grid=(b, n_steps, t // tq),
        in_specs=in_specs,
        out_specs=pl.BlockSpec((1, tq, w), lambda bi, h, i: (bi, i, h)),
        out_shape=jax.ShapeDtypeStruct((b, t, n_heads * LANES), BF16),
        compiler_params=_cparams(),
        name=name,
    )(*args)


def _na_kernel(*refs, n_local):
    q_ref, kc_ref, vc_ref = refs[:3]
    k_refs = refs[3:3 + n_local]
    v_refs = refs[3 + n_local:3 + 2 * n_local]
    bias_ref = refs[3 + 2 * n_local] if n_local else None
    o_ref = refs[-1]
    tq = q_ref.shape[1]
    n_pairs = q_ref.shape[2] // LANES
    streams = []
    for p in range(n_pairs):
        cols = slice(p * LANES, (p + 1) * LANES)
        q_stack = jnp.concatenate(_split_heads_rows(q_ref[0, :, cols]), axis=0)
        chunks = [(kc_ref[0, :, cols], vc_ref[0, cols, :], None)]
        for c in range(n_local):
            chunks.append((k_refs[c][0, :, cols], v_refs[c][0, cols, :], bias_ref[0, p, c * tq:(c + 1) * tq, :]))
        streams.append((q_stack, chunks))
    for p, res in enumerate(_attend(streams)):
        o_ref[0, :, p * LANES:(p + 1) * LANES] = _pick_head_rows(res, tq, 0).astype(BF16)


def _na_call(qk_lat, vt_lat, qk_ctx, vt_ctx, bias_tab, name):
    b, s, _ = qk_lat.shape
    n_ctx = qk_ctx.shape[1]
    tq = NA_ROWS_PER_BLOCK * GRID_W
    nj = s // tq
    n_pairs = bias_tab.shape[1]
    w = n_pairs * LANES
    n_local = 3

    def tb(j):
        return jnp.clip(j - 1, 0, nj - n_local)

    def cls(j):
        return jnp.where(j == 0, 0, jnp.where(j == nj - 1, 2, 1))

    k_specs = [pl.BlockSpec((1, tq, w), functools.partial(lambda j, bi, d: (bi, tb(j) + d, 1), d=d))
               for d in range(n_local)]
    v_specs = [pl.BlockSpec((1, w, tq), functools.partial(lambda j, bi, d: (bi, 0, tb(j) + d), d=d))
               for d in range(n_local)]
    return pl.pallas_call(
        functools.partial(_na_kernel, n_local=n_local),
        grid=(nj, b),
        in_specs=[
            pl.BlockSpec((1, tq, w), lambda j, bi: (bi, j, 0)),
            pl.BlockSpec((1, n_ctx, w), lambda j, bi: (bi, 0, 1)),
            pl.BlockSpec((1, w, n_ctx), lambda j, bi: (bi, 0, 0)),
            *k_specs, *v_specs,
            pl.BlockSpec((1, n_pairs, n_local * tq, 2 * tq), lambda j, bi: (cls(j), 0, 0, 0)),
        ],
        out_specs=pl.BlockSpec((1, tq, w), lambda j, bi: (bi, j, 0)),
        out_shape=jax.ShapeDtypeStruct((b, s, w), BF16),
        compiler_params=_cparams(),
        name=name,
    )(qk_lat, qk_ctx, vt_ctx, *([qk_lat] * n_local), *([vt_lat] * n_local), bias_tab)


def _na_ctx_call(qk_ctx, vt_ctx, n_pairs, name):
    b, n_ctx, _ = qk_ctx.shape
    w = n_pairs * LANES
    return pl.pallas_call(
        functools.partial(_na_kernel, n_local=0),
        grid=(b,),
        in_specs=[
            pl.BlockSpec((1, n_ctx, w), lambda bi: (bi, 0, 0)),
            pl.BlockSpec((1, n_ctx, w), lambda bi: (bi, 0, 1)),
            pl.BlockSpec((1, w, n_ctx), lambda bi: (bi, 0, 0)),
        ],
        out_specs=pl.BlockSpec((1, n_ctx, w), lambda bi: (bi, 0, 0)),
        out_shape=jax.ShapeDtypeStruct((b, n_ctx, w), BF16),
        compiler_params=_cparams(),
        name=name,
    )(qk_ctx, qk_ctx, vt_ctx)


def _na_bias_tables(rpb, seq):
    n_heads = rpb.shape[0]
    rows = seq // GRID_W
    tq = NA_ROWS_PER_BLOCK * GRID_W
    nj = seq // tq
    kh = min(NA_KH, rows)
    n_dc = rpb.shape[2]
    pad = GRID_W - 1 - (n_dc - 1) // 2
    rpb_pad = jnp.pad(rpb.astype(F32), ((0, 0), (0, 0), (pad, pad)), mode="edge")
    col_tab = jnp.stack([rpb_pad[:, :, GRID_W - 1 - qc:2 * GRID_W - 1 - qc] for qc in range(GRID_W)], axis=-1)
    tabs = []
    for j in (0, 1, nj - 1):
        ws = NA_ROWS_PER_BLOCK * int(np.clip(j - 1, 0, nj - 3))
        n_kr = 3 * NA_ROWS_PER_BLOCK
        kr = ws + np.arange(n_kr)
        r = NA_ROWS_PER_BLOCK * j + np.arange(NA_ROWS_PER_BLOCK)
        dr = np.clip(kr[:, None] - r[None, :] + NA_KH - 1, 0, 2 * NA_KH - 2)
        blocks = jnp.stack([jnp.stack([col_tab[:, dr[a, q]] for q in range(NA_ROWS_PER_BLOCK)], axis=1)
                            for a in range(n_kr)], axis=1)
        vals = blocks.transpose(0, 1, 3, 2, 4).reshape(n_heads, 3 * tq, tq)
        kk = np.arange(3 * tq)
        kr_f, kc = ws + kk // GRID_W, kk % GRID_W
        qq = np.arange(tq)
        r_f, cq = NA_ROWS_PER_BLOCK * j + qq // GRID_W, qq % GRID_W
        rs = np.clip(r_f - kh // 2, 0, rows - kh)
        cs = np.clip(cq - NA_KW // 2, 0, GRID_W - NA_KW)
        ok = ((kr_f[:, None] >= rs[None, :]) & (kr_f[:, None] < rs[None, :] + kh)
              & (kc[:, None] >= cs[None, :]) & (kc[:, None] < cs[None, :] + NA_KW))
        tabs.append(jnp.where(ok[None], vals, -jnp.inf))
    tab = jnp.stack(tabs)
    tab = tab.reshape(3, n_heads // 2, 2, 3 * tq, tq).transpose(0, 1, 3, 2, 4)
    return tab.reshape(3, n_heads // 2, 3 * tq, 2 * tq)


def _outproj_kernel(*refs, n_att):
    att_refs = refs[:n_att]
    x_ref, mod_ref, w_ref, lng_ref, lnb_ref, o_ref = refs[n_att:]
    y = None
    k0 = 0
    for a_ref in att_refs:
        kw = a_ref.shape[2]
        part = jnp.dot(a_ref[0], w_ref[k0:k0 + kw, :], preferred_element_type=F32)
        y = part if y is None else y + part
        k0 += kw
    z = ALPHA * x_ref[0] + mod_ref[0][2:3] * y
    o_ref[0] = _layer_norm(z, lng_ref[...], lnb_ref[...])


def _outproj_call(atts, x, mod, mod_row, w_o, ln_g, ln_b, tm, name):
    b, t, d = x.shape
    mod_map = (lambda bi, i: (bi, 0, 0)) if mod_row is None else (lambda bi, i: (mod_row, 0, 0))
    in_specs = [pl.BlockSpec((1, tm, a.shape[2]), lambda bi, i: (bi, i, 0)) for a in atts]
    in_specs += [
        pl.BlockSpec((1, tm, d), lambda bi, i: (bi, i, 0)),
        pl.BlockSpec((1,) + mod.shape[1:], mod_map),
        _const_spec(w_o.shape),
        _const_spec((1, d)),
        _const_spec((1, d)),
    ]
    return pl.pallas_call(
        functools.partial(_outproj_kernel, n_att=len(atts)),
        grid=(b, t // tm),
        in_specs=in_specs,
        out_specs=pl.BlockSpec((1, tm, d), lambda bi, i: (bi, i, 0)),
        out_shape=jax.ShapeDtypeStruct((b, t, d), F32),
        compiler_params=_cparams(),
        name=name,
    )(*atts, x, mod, w_o, ln_g.reshape(1, d), ln_b.reshape(1, d))


def _ffn_kernel(*refs, tm, halo):
    if halo:
        (x_ref, xp_ref, xn_ref, mod_ref, wup_ref, cw_ref, cb_ref, wdn_ref, lng_ref, lnb_ref,
         o_ref, hs_ref, act_ref) = refs
    else:
        x_ref, mod_ref, wup_ref, cw_ref, cb_ref, wdn_ref, lng_ref, lnb_ref, o_ref, hs_ref, act_ref = refs
    d = x_ref.shape[2]
    d_ff = wdn_ref.shape[0]
    rows = tm + HALO
    mod = mod_ref[0]
    shift, scale, gate = mod[3:4], mod[4:5], mod[5:6]
    x = x_ref[0]
    hs_ref[0:tm, :] = (x * (1.0 + scale) + shift).astype(BF16)
    zeros8 = jnp.zeros((SUBLANES_F32, d), F32)
    if halo:
        i = pl.program_id(1)
        last = pl.num_programs(1) - 1
        hp = jnp.where(i > 0, xp_ref[0] * (1.0 + scale) + shift, 0.0)
        hn = jnp.where(i < last, xn_ref[0] * (1.0 + scale) + shift, 0.0)
    else:
        hp = hn = zeros8
    hs_ref[tm:rows, :] = jnp.concatenate([hn, hp], axis=0).astype(BF16)
    hs = hs_ref[...]
    n_chunks = d_ff // FF_CHUNK

    def up_proj(cols):
        return jnp.dot(hs, wup_ref[:, cols], preferred_element_type=F32)

    def conv(up, cols):
        w = cw_ref[:, cols]
        return (pltpu.roll(up, 1, 0)[0:tm] * w[0:1] + up[0:tm] * w[1:2]
                + pltpu.roll(up, rows - 1, 0)[0:tm] * w[2:3] + cb_ref[:, cols])

    ups = {}
    for t in range(n_chunks + 1):
        if t < n_chunks:
            a_cols = slice(t * FF_CHUNK, (t + 1) * FF_CHUNK)
            g_cols = slice(d_ff + t * FF_CHUNK, d_ff + (t + 1) * FF_CHUNK)
            ups[t] = (up_proj(a_cols), up_proj(g_cols), a_cols, g_cols)
        if t >= 1:
            up_a, up_g, a_cols, g_cols = ups.pop(t - 1)
            act_ref[:, a_cols] = (_gelu_tanh(conv(up_g, g_cols)) * conv(up_a, a_cols)).astype(BF16)
    acc = jnp.dot(act_ref[...], wdn_ref[...], preferred_element_type=F32)
    z = ALPHA * x + gate * acc
    o_ref[0] = _layer_norm(z, lng_ref[...], lnb_ref[...])


def _ffn_call(x, mod, mod_row, w_up, conv_w, conv_b, w_down, ln_g, ln_b, tm, name):
    b, t, d = x.shape
    halo = t > tm
    mod_map = (lambda bi, i: (bi, 0, 0)) if mod_row is None else (lambda bi, i: (mod_row, 0, 0))
    in_specs = [pl.BlockSpec((1, tm, d), lambda bi, i: (bi, i, 0))]
    args = [x]
    if halo:
        r8 = tm // SUBLANES_F32
        n8 = t // SUBLANES_F32
        in_specs += [
            pl.BlockSpec((1, SUBLANES_F32, d), lambda bi, i: (bi, jnp.maximum(i * r8 - 1, 0), 0)),
            pl.BlockSpec((1, SUBLANES_F32, d), lambda bi, i: (bi, jnp.minimum((i + 1) * r8, n8 - 1), 0)),
        ]
        args += [x, x]
    in_specs += [
        pl.BlockSpec((1,) + mod.shape[1:], mod_map),
        _const_spec(w_up.shape),
        _const_spec(conv_w.shape),
        _const_spec(conv_b.shape),
        _const_spec(w_down.shape),
        _const_spec((1, d)),
        _const_spec((1, d)),
    ]
    args += [mod, w_up, conv_w, conv_b, w_down, ln_g.reshape(1, d), ln_b.reshape(1, d)]
    return pl.pallas_call(
        functools.partial(_ffn_kernel, tm=tm, halo=halo),
        grid=(b, t // tm),
        in_specs=in_specs,
        out_specs=pl.BlockSpec((1, tm, d), lambda bi, i: (bi, i, 0)),
        out_shape=jax.ShapeDtypeStruct((b, t, d), F32),
        scratch_shapes=[pltpu.VMEM((tm + HALO, d), BF16), pltpu.VMEM((tm, w_down.shape[0]), BF16)],
        compiler_params=_cparams(),
        name=name,
    )(*args)


def _rope_tables(seq):
    n_freq = HEAD_DIM // 4
    inv = ROPE_BASE ** (-jnp.arange(n_freq, dtype=F32) / n_freq)
    t = jnp.arange(seq)
    row = (t // GRID_W).astype(F32)
    col = (t % GRID_W).astype(F32)
    ar = row[:, None] * inv
    ac = col[:, None] * inv
    ang = jnp.concatenate([ar, ar, ac, ac], -1)
    cos, sin = jnp.cos(ang), jnp.sin(ang)
    even_chunk = ((np.arange(HEAD_DIM) // n_freq) % 2 == 0)[None, :]
    sin_a = jnp.where(even_chunk, -sin, 0.0)
    sin_b = jnp.where(even_chunk, 0.0, sin)
    return tuple(jnp.concatenate([a, a], axis=1) for a in (cos, sin_a, sin_b))


def _dup_heads(w, n_heads):
    d = w.shape[0]
    w = w.reshape(d, n_heads, HEAD_DIM)
    return jnp.concatenate([w, w], axis=-1).reshape(d, n_heads * PAIR)


def kernel(x, c, ctx, c_ctx, w_ada, b_ada, ln_g, ln_b, w_in_ab, w_o_ab, na_rpb, diff_lambda, diff_subln,
           w_in_c, w_o_c, gqa_qk_norm, w_up, conv_w, conv_b, w_down):
    b, s, d = x.shape
    n_ctx = ctx.shape[1]
    d_ff = w_down.shape[1]
    na_w = na_rpb.shape[1] * HEAD_DIM
    diff_w = d - na_w
    gqa_kw = GQA_KV_HEADS * HEAD_DIM
    tm_lat = min(TM_LATENT, s)

    mod_rows = -(-(b + 1) // SUBLANES_F32) * SUBLANES_F32
    ctx_row = b
    cc = jnp.concatenate([c, c_ctx[None, :], jnp.zeros((mod_rows - b - 1, d), F32)], axis=0)
    mod_all = _ada_call(cc, w_ada, b_ada)
    rope_tabs = _rope_tables(s)
    ones_gain = jnp.ones((2, LANES), F32)

    xc = ctx
    for l in range(DEPTH):
        need_ctx = l < DEPTH - 1
        i = l // 2
        mod = mod_all[l].reshape(mod_rows, 6, d)
        if l % 2 == 0:
            lambda_init = 0.8 - 0.6 * math.exp(-0.3 * l)
            w = w_in_ab[i]
            o0 = 3 * na_w
            wqk = jnp.concatenate([w[:, :2 * na_w], w[:, o0:o0 + 2 * diff_w]], axis=1).astype(BF16)
            wvt = jnp.concatenate([w[:, 2 * na_w:o0], w[:, o0 + 2 * diff_w:]], axis=1).T.astype(BF16)
            nb = na_w // LANES
            plan = ([(None, False, Q_SCALE)] * nb + [(None, False, 1.0)] * nb
                    + [(None, True, Q_SCALE)] * (diff_w // LANES) + [(None, True, 1.0)] * (diff_w // LANES))
            gains = ones_gain
        else:
            w = w_in_c[i]
            q_w = d
            wqk = jnp.concatenate([w[:, :q_w], _dup_heads(w[:, q_w:q_w + gqa_kw], GQA_KV_HEADS)],
                                  axis=1).astype(BF16)
            wvt = w[:, q_w + gqa_kw:].T.astype(BF16)
            plan = [(0, True, Q_SCALE)] * (q_w // LANES) + [(1, True, 1.0)] * GQA_KV_HEADS
            gains = jnp.concatenate([gqa_qk_norm[i], gqa_qk_norm[i]], axis=1)
        plan = tuple(plan)

        qk_lat, vt_lat = _inproj_call(x, mod, None, wqk, wvt, gains, plan, rope_tabs, tm_lat, f"inproj_lat_{l}")
        qk_ctx, vt_ctx = _inproj_call(xc, mod, ctx_row, wqk, wvt, gains, plan, None, n_ctx, f"inproj_ctx_{l}")

        if l % 2 == 0:
            bias_tab = _na_bias_tables(na_rpb[i], s) * LOG2E
            atts = [
                _na_call(qk_lat, vt_lat, qk_ctx, vt_ctx, bias_tab, f"na_lat_{l}"),
                _diff_call(qk_lat, qk_ctx, vt_ctx, qk_lat, vt_lat, diff_lambda[i], diff_subln[i], lambda_init,
                           min(TQ_DIFF, s), 1, f"diff_lat_{l}"),
            ]
            w_o = w_o_ab[i].astype(BF16)
            if need_ctx:
                atts_c = [
                    _na_ctx_call(qk_ctx, vt_ctx, na_w // LANES, f"na_ctx_{l}"),
                    _diff_call(qk_ctx, qk_ctx, vt_ctx, None, None, diff_lambda[i], diff_subln[i], lambda_init,
                               n_ctx, diff_w // LANES, f"diff_ctx_{l}"),
                ]
        else:
            atts = [_gqa_call(qk_lat, qk_ctx, vt_ctx, qk_lat, vt_lat, min(TQ_GQA, s), 1, f"gqa_lat_{l}")]
            w_o = w_o_c[i].astype(BF16)
            if need_ctx:
                atts_c = [_gqa_call(qk_ctx, qk_ctx, vt_ctx, None, None, n_ctx, GQA_KV_HEADS, f"gqa_ctx_{l}")]

        w_up_l = w_up[l].astype(BF16)
        conv_w_l = conv_w[l]
        conv_b_l = conv_b[l].reshape(1, 2 * d_ff)
        w_down_l = w_down[l].astype(BF16)

        x = _outproj_call(atts, x, mod, None, w_o, ln_g[l, 0], ln_b[l, 0], tm_lat, f"outproj_lat_{l}")
        x = _ffn_call(x, mod, None, w_up_l, conv_w_l, conv_b_l, w_down_l, ln_g[l, 1], ln_b[l, 1],
                      tm_lat, f"ffn_lat_{l}")
        if need_ctx:
            xc = _outproj_call(atts_c, xc, mod, ctx_row, w_o, ln_g[l, 0], ln_b[l, 0], n_ctx, f"outproj_ctx_{l}")
            xc = _ffn_call(xc, mod, ctx_row, w_up_l, conv_w_l, conv_b_l, w_down_l, ln_g[l, 1], ln_b[l, 1],
                           n_ctx, f"ffn_ctx_{l}")
    return x
```

```python
import functools
import math

import numpy as np
import jax
import jax.numpy as jnp
from jax import lax
from jax.experimental import pallas as pl
from jax.experimental.pallas import tpu as pltpu

F32 = jnp.float32
BF16 = jnp.bfloat16

DEPTH = 4
GRID_W = 64
HEAD_DIM = 64
NA_KH = 8
NA_KW = 16
GQA_KV_HEADS = 4
CONV_WIDTH = 3
ROPE_BASE = 10000.0
EPS = 1e-6
ALPHA = (2 * DEPTH) ** 0.25
LOG2E = math.log2(math.e)
Q_SCALE = HEAD_DIM ** -0.5 * LOG2E

LANES = 128
SUBLANES_F32 = 8
SUBLANES_BF16 = 16
MXU_DIM = 256
VMEM_LIMIT_BYTES = 56 * 1024 * 1024

PAIR = 2 * HEAD_DIM
assert PAIR == LANES

TM_LATENT = 512
TQ_GQA = 512
TQ_DIFF = 1024
NA_ROWS_PER_BLOCK = 4
KEY_CHUNK = MXU_DIM
ONES_ROWS = SUBLANES_BF16
FF_CHUNK = MXU_DIM
ADA_TN = 1536
INPROJ_GROUP = MXU_DIM
HALO = SUBLANES_BF16
FFN_OUT_ROW_BLOCKS = 2


def _cparams():
    return pltpu.CompilerParams(vmem_limit_bytes=VMEM_LIMIT_BYTES)


def _const_spec(shape):
    nd = len(shape)
    return pl.BlockSpec(shape, lambda *_: (0,) * nd, pipeline_mode=pl.Buffered(1))


def _layer_norm(z, g, b):
    mu = jnp.mean(z, axis=-1, keepdims=True)
    d = z - mu
    var = jnp.mean(d * d, axis=-1, keepdims=True)
    return d * lax.rsqrt(var + EPS) * g + b


def _gelu_tanh(x):
    c = math.sqrt(2.0 / math.pi)
    return (0.5 * x) * (1.0 + jnp.tanh(x * (c + (0.044715 * c) * (x * x))))


def _group_sum_matrix():
    r = lax.broadcasted_iota(jnp.int32, (2 * LANES, LANES), 0)
    c = lax.broadcasted_iota(jnp.int32, (2 * LANES, LANES), 1)
    same = ((r % LANES) // HEAD_DIM) == (c // HEAD_DIM)
    return jnp.where(same, 1.0, 0.0).astype(BF16)


def _head_rms(xb, gm, g):
    x2 = xb * xb
    hi = x2.astype(BF16)
    lo = (x2 - hi.astype(F32)).astype(BF16)
    ssq = jnp.dot(jnp.concatenate([hi, lo], axis=1), gm, preferred_element_type=F32)
    return xb * lax.rsqrt(ssq * (1.0 / HEAD_DIM) + EPS) * g


def _rope(xb, cos, sin_a, sin_b):
    return (xb * cos
            + pltpu.roll(xb, LANES - HEAD_DIM // 4, 1) * sin_a
            + pltpu.roll(xb, HEAD_DIM // 4, 1) * sin_b)


def _split_heads_rows(q_pair):
    qf = q_pair.astype(F32)
    lane = lax.broadcasted_iota(jnp.int32, qf.shape, 1)
    lo = lane < HEAD_DIM
    return [jnp.where(lo, qf, 0.0).astype(BF16), jnp.where(lo, 0.0, qf).astype(BF16)]


def _attend(streams):
    n_chunks = len(streams[0][1])
    s_buf, e_buf = {}, {}
    m = [None] * len(streams)
    acc = [None] * len(streams)
    for t in range(n_chunks + 2):
        for i, (q_stack, chunks) in enumerate(streams):
            if t < n_chunks:
                k_c, _, bias_c = chunks[t]
                s = lax.dot_general(k_c, q_stack, (((1,), (1,)), ((), ())), preferred_element_type=F32)
                s_buf[i, t] = s if bias_c is None else s + bias_c
        for i, (q_stack, chunks) in enumerate(streams):
            if 0 <= t - 1 < n_chunks:
                s = s_buf.pop((i, t - 1))
                mc = jnp.max(s, axis=0, keepdims=True)
                m_new = mc if m[i] is None else jnp.maximum(m[i], mc)
                alpha = None if m[i] is None else jnp.exp2(m[i] - m_new)
                m[i] = m_new
                e_buf[i, t - 1] = (jnp.exp2(s - m_new).astype(BF16), alpha)
        for i, (q_stack, chunks) in enumerate(streams):
            if 0 <= t - 2 < n_chunks:
                e, alpha = e_buf.pop((i, t - 2))
                vt_c = chunks[t - 2][1]
                vt_aug = jnp.concatenate([vt_c, jnp.ones((ONES_ROWS, vt_c.shape[1]), BF16)], axis=0)
                pv = jnp.dot(vt_aug, e, preferred_element_type=F32)
                acc[i] = pv if alpha is None else alpha * acc[i] + pv
    outs = []
    for (q_stack, chunks), a in zip(streams, acc):
        dv = chunks[0][1].shape[0]
        outs.append(a[:dv] * (1.0 / a[dv:dv + 1]))
    return outs


def _key_chunks(kc_ref, vc_ref, kl_ref, vl_ref, k_cols=slice(None), v_rows=slice(None)):
    chunks = [(kc_ref[0, :, k_cols], vc_ref[0, v_rows, :], None)]
    if kl_ref is not None:
        n_lat = kl_ref.shape[1]
        for c0 in range(0, n_lat, KEY_CHUNK):
            chunks.append((kl_ref[0, c0:c0 + KEY_CHUNK, k_cols], vl_ref[0, v_rows, c0:c0 + KEY_CHUNK], None))
    return chunks


def _pick_head_rows(res, tq, p):
    row = lax.broadcasted_iota(jnp.int32, (LANES, tq), 0)
    a = res[:, (2 * p) * tq:(2 * p + 1) * tq]
    b = res[:, (2 * p + 1) * tq:(2 * p + 2) * tq]
    return jnp.where(row < HEAD_DIM, a, b).T


def _stack_head_rows(res, tq, p):
    return jnp.concatenate([res[:, (2 * p) * tq:(2 * p + 1) * tq],
                            res[:, (2 * p + 1) * tq:(2 * p + 2) * tq]], axis=0).T


def _ada_kernel(c_ref, w_ref, b_ref, o_ref):
    c = c_ref[...]
    cond = c * (1.0 / (1.0 + jnp.exp(-c)))
    o_ref[0] = jnp.dot(cond, w_ref[0], precision=lax.Precision.HIGHEST,
                       preferred_element_type=F32) + b_ref[0]


def _ada_call(cc, w_ada, b_ada):
    n_layers, d, n_out = w_ada.shape
    rows = cc.shape[0]
    return pl.pallas_call(
        _ada_kernel,
        grid=(n_layers, n_out // ADA_TN),
        in_specs=[
            pl.BlockSpec((rows, d), lambda l, n: (0, 0)),
            pl.BlockSpec((1, d, ADA_TN), lambda l, n: (l, 0, n)),
            pl.BlockSpec((1, 1, ADA_TN), lambda l, n: (l, 0, n)),
        ],
        out_specs=pl.BlockSpec((1, rows, ADA_TN), lambda l, n: (l, 0, n)),
        out_shape=jax.ShapeDtypeStruct((n_layers, rows, n_out), F32),
        compiler_params=_cparams(),
        name="adaln_mod",
    )(cc, w_ada, b_ada.reshape(n_layers, 1, n_out))


def _inproj_kernel(*refs, block_plan, rope):
    if rope:
        x_ref, mod_ref, wqk_ref, wvt_ref, g_ref, cos_ref, sa_ref, sb_ref, oqk_ref, ovt_ref = refs
    else:
        x_ref, mod_ref, wqk_ref, wvt_ref, g_ref, oqk_ref, ovt_ref = refs
    mod = mod_ref[0]
    h = (x_ref[0] * (1.0 + mod[1:2]) + mod[0:1]).astype(BF16)
    has_norm = any(p[0] is not None for p in block_plan)
    gm = _group_sum_matrix() if has_norm else None
    if rope:
        cos, sa, sb = cos_ref[...], sa_ref[...], sb_ref[...]

    def v_proj():
        vt = lax.dot_general(wvt_ref[...], h, (((1,), (1,)), ((), ())), preferred_element_type=F32)
        ovt_ref[0] = vt.astype(BF16)

    def epilogue(j0, qk):
        for jj in range(qk.shape[1] // LANES):
            gain_row, do_rope, scale = block_plan[j0 + jj]
            blk = qk[:, jj * LANES:(jj + 1) * LANES]
            if gain_row is not None:
                blk = _head_rms(blk, gm, g_ref[gain_row:gain_row + 1, :])
            if do_rope and rope:
                blk = _rope(blk, cos, sa, sb)
            if scale != 1.0:
                blk = blk * scale
            oqk_ref[0, :, (j0 + jj) * LANES:(j0 + jj + 1) * LANES] = blk.astype(BF16)

    if not has_norm:
        qk = jnp.dot(h, wqk_ref[...], preferred_element_type=F32)
        v_proj()
        epilogue(0, qk)
        return
    bpg = INPROJ_GROUP // LANES
    n_groups = len(block_plan) // bpg
    prev = None
    for g in range(n_groups + 1):
        cur = None
        if g < n_groups:
            cur = jnp.dot(h, wqk_ref[:, g * INPROJ_GROUP:(g + 1) * INPROJ_GROUP], preferred_element_type=F32)
        else:
            v_proj()
        if prev is not None:
            epilogue((g - 1) * bpg, prev)
        prev = cur


def _inproj_call(x, mod, mod_row, wqk, wvt, gains, block_plan, rope_tabs, tm, name):
    b, t, d = x.shape
    wq = wqk.shape[1]
    wv = wvt.shape[0]
    rope = rope_tabs is not None
    mod_map = (lambda bi, i: (bi, 0, 0)) if mod_row is None else (lambda bi, i: (mod_row, 0, 0))
    in_specs = [
        pl.BlockSpec((1, tm, d), lambda bi, i: (bi, i, 0)),
        pl.BlockSpec((1,) + mod.shape[1:], mod_map),
        _const_spec(wqk.shape),
        _const_spec(wvt.shape),
        _const_spec(gains.shape),
    ]
    args = [x, mod, wqk, wvt, gains]
    if rope:
        in_specs += [pl.BlockSpec((tm, LANES), lambda bi, i: (i, 0))] * 3
        args += list(rope_tabs)
    return pl.pallas_call(
        functools.partial(_inproj_kernel, block_plan=block_plan, rope=rope),
        grid=(b, t // tm),
        in_specs=in_specs,
        out_specs=[
            pl.BlockSpec((1, tm, wq), lambda bi, i: (bi, i, 0)),
            pl.BlockSpec((1, wv, tm), lambda bi, i: (bi, 0, i)),
        ],
        out_shape=[jax.ShapeDtypeStruct((b, t, wq), BF16), jax.ShapeDtypeStruct((b, wv, t), BF16)],
        compiler_params=_cparams(),
        name=name,
    )(*args)


def _gqa_kernel(*refs, tq, latent_keys):
    if latent_keys:
        q_ref, kc_ref, vc_ref, kl_ref, vl_ref, o_ref = refs
    else:
        q_ref, kc_ref, vc_ref, o_ref = refs
        kl_ref = vl_ref = None
    n_groups = q_ref.shape[2] // (2 * LANES)
    streams = []
    for g in range(n_groups):
        q = q_ref[0, :, g * 2 * LANES:(g + 1) * 2 * LANES]
        parts = _split_heads_rows(q[:, :LANES]) + _split_heads_rows(q[:, LANES:])
        chunks = _key_chunks(kc_ref, vc_ref, kl_ref, vl_ref, slice(g * LANES, (g + 1) * LANES),
                             slice(g * HEAD_DIM, (g + 1) * HEAD_DIM))
        streams.append((jnp.concatenate(parts, axis=0), chunks))
    for g, res in enumerate(_attend(streams)):
        o_ref[0, :, g * 2 * LANES:(g + 1) * 2 * LANES] = jnp.concatenate(
            [_stack_head_rows(res, tq, 0), _stack_head_rows(res, tq, 1)], axis=1).astype(BF16)


def _gqa_call(qk_q, qk_ctx, vt_ctx, qk_lat, vt_lat, tq, groups, name):
    b, t, _ = qk_q.shape
    n_ctx = qk_ctx.shape[1]
    k_blk0 = (GQA_KV_HEADS * 4 * HEAD_DIM) // (groups * LANES)
    latent_keys = qk_lat is not None
    in_specs = [
        pl.BlockSpec((1, tq, groups * 2 * LANES), lambda bi, g, i: (bi, i, g)),
        pl.BlockSpec((1, n_ctx, groups * LANES), lambda bi, g, i: (bi, 0, k_blk0 + g)),
        pl.BlockSpec((1, groups * HEAD_DIM, n_ctx), lambda bi, g, i: (bi, g, 0)),
    ]
    args = [qk_q, qk_ctx, vt_ctx]
    if latent_keys:
        n_lat = qk_lat.shape[1]
        in_specs += [
            pl.BlockSpec((1, n_lat, groups * LANES), lambda bi, g, i: (bi, 0, k_blk0 + g)),
            pl.BlockSpec((1, groups * HEAD_DIM, n_lat), lambda bi, g, i: (bi, g, 0)),
        ]
        args += [qk_lat, vt_lat]
    return pl.pallas_call(
        functools.partial(_gqa_kernel, tq=tq, latent_keys=latent_keys),
        grid=(b, GQA_KV_HEADS // groups, t // tq),
        in_specs=in_specs,
        out_specs=pl.BlockSpec((1, tq, groups * 2 * LANES), lambda bi, g, i: (bi, i, g)),
        out_shape=jax.ShapeDtypeStruct((b, t, GQA_KV_HEADS * 2 * LANES), BF16),
        compiler_params=_cparams(),
        name=name,
    )(*args)


def _diff_kernel(*refs, tq, latent_keys, lambda_init):
    if latent_keys:
        q_ref, kc_ref, vc_ref, kl_ref, vl_ref, lam_ref, g_ref, o_ref = refs
    else:
        q_ref, kc_ref, vc_ref, lam_ref, g_ref, o_ref = refs
        kl_ref = vl_ref = None
    n_heads = q_ref.shape[2] // LANES
    streams = []
    for hd in range(n_heads):
        cols = slice(hd * LANES, (hd + 1) * LANES)
        q_stack = jnp.concatenate(_split_heads_rows(q_ref[0, :, cols]), axis=0)
        streams.append((q_stack, _key_chunks(kc_ref, vc_ref, kl_ref, vl_ref, cols, cols)))
    lv = lam_ref[...]
    lam = (jnp.exp(jnp.sum(lv[0:1] * lv[1:2], axis=1, keepdims=True))
           - jnp.exp(jnp.sum(lv[2:3] * lv[3:4], axis=1, keepdims=True)) + lambda_init)
    for hd, res in enumerate(_attend(streams)):
        o = (res[:, :tq] - lam * res[:, tq:]).T
        o = o * lax.rsqrt(jnp.mean(o * o, axis=-1, keepdims=True) + EPS) * g_ref[...]
        o_ref[0, :, hd * LANES:(hd + 1) * LANES] = (o * (1.0 - lambda_init)).astype(BF16)


def _diff_call(qk_q, qk_ctx, vt_ctx, qk_lat, vt_lat, lam_vec, subln_g, lambda_init, tq, heads, name):
    b, t, wqk = qk_q.shape
    n_ctx = qk_ctx.shape[1]
    n_heads = wqk // (4 * LANES)
    n_steps = n_heads // heads
    q_blk0, k_blk0, v_blk0 = 2 * n_steps, 3 * n_steps, n_steps
    w = heads * LANES
    latent_keys = qk_lat is not None
    in_specs = [
        pl.BlockSpec((1, tq, w), lambda bi, h, i: (bi, i, q_blk0 + h)),
        pl.BlockSpec((1, n_ctx, w), lambda bi, h, i: (bi, 0, k_blk0 + h)),
        pl.BlockSpec((1, w, n_ctx), lambda bi, h, i: (bi, v_blk0 + h, 0)),
    ]
    args = [qk_q, qk_ctx, vt_ctx]
    if latent_keys:
        n_lat = qk_lat.shape[1]
        in_specs += [
            pl.BlockSpec((1, n_lat, w), lambda bi, h, i: (bi, 0, k_blk0 + h)),
            pl.BlockSpec((1, w, n_lat), lambda bi, h, i: (bi, v_blk0 + h, 0)),
        ]
        args += [qk_lat, vt_lat]
    in_specs += [_const_spec(lam_vec.shape), _const_spec((1, LANES))]
    args += [lam_vec, subln_g.reshape(1, LANES)]
    return pl.pallas_call(
        functools.partial(_diff_kernel, tq=tq, latent_keys=latent_keys, lambda_init=lambda_init),
        grid=(b, n_steps, t // tq),
        in_specs=in_specs,
        out_specs=pl.BlockSpec((1, tq, w), lambda bi, h, i: (bi, i, h)),
        out_shape=jax.ShapeDtypeStruct((b, t, n_heads * LANES), BF16),
        compiler_params=_cparams(),
        name=name,
    )(*args)


def _na_kernel(*refs, n_local):
    q_ref, kc_ref, vc_ref = refs[:3]
    k_refs = refs[3:3 + n_local]
    v_refs = refs[3 + n_local:3 + 2 * n_local]
    bias_ref = refs[3 + 2 * n_local] if n_local else None
    o_ref = refs[-1]
    tq = q_ref.shape[1]
    n_pairs = q_ref.shape[2] // LANES
    streams = []
    for p in range(n_pairs):
        cols = slice(p * LANES, (p + 1) * LANES)
        q_stack = jnp.concatenate(_split_heads_rows(q_ref[0, :, cols]), axis=0)
        chunks = [(kc_ref[0, :, cols], vc_ref[0, cols, :], None)]
        for c in range(n_local):
            chunks.append((k_refs[c][0, :, cols], v_refs[c][0, cols, :], bias_ref[0, p, c * tq:(c + 1) * tq, :]))
        streams.append((q_stack, chunks))
    for p, res in enumerate(_attend(streams)):
        o_ref[0, :, p * LANES:(p + 1) * LANES] = _pick_head_rows(res, tq, 0).astype(BF16)


def _na_call(qk_lat, vt_lat, qk_ctx, vt_ctx, bias_tab, name):
    b, s, _ = qk_lat.shape
    n_ctx = qk_ctx.shape[1]
    tq = NA_ROWS_PER_BLOCK * GRID_W
    nj = s // tq
    n_pairs = bias_tab.shape[1]
    w = n_pairs * LANES
    n_local = 3

    def tb(j):
        return jnp.clip(j - 1, 0, nj - n_local)

    def cls(j):
        return jnp.where(j == 0, 0, jnp.where(j == nj - 1, 2, 1))

    k_specs = [pl.BlockSpec((1, tq, w), functools.partial(lambda j, bi, d: (bi, tb(j) + d, 1), d=d))
               for d in range(n_local)]
    v_specs = [pl.BlockSpec((1, w, tq), functools.partial(lambda j, bi, d: (bi, 0, tb(j) + d), d=d))
               for d in range(n_local)]
    return pl.pallas_call(
        functools.partial(_na_kernel, n_local=n_local),
        grid=(nj, b),
        in_specs=[
            pl.BlockSpec((1, tq, w), lambda j, bi: (bi, j, 0)),
            pl.BlockSpec((1, n_ctx, w), lambda j, bi: (bi, 0, 1)),
            pl.BlockSpec((1, w, n_ctx), lambda j, bi: (bi, 0, 0)),
            *k_specs, *v_specs,
            pl.BlockSpec((1, n_pairs, n_local * tq, 2 * tq), lambda j, bi: (cls(j), 0, 0, 0)),
        ],
        out_specs=pl.BlockSpec((1, tq, w), lambda j, bi: (bi, j, 0)),
        out_shape=jax.ShapeDtypeStruct((b, s, w), BF16),
        compiler_params=_cparams(),
        name=name,
    )(qk_lat, qk_ctx, vt_ctx, *([qk_lat] * n_local), *([vt_lat] * n_local), bias_tab)


def _na_ctx_call(qk_ctx, vt_ctx, n_pairs, name):
    b, n_ctx, _ = qk_ctx.shape
    w = n_pairs * LANES
    return pl.pallas_call(
        functools.partial(_na_kernel, n_local=0),
        grid=(b,),
        in_specs=[
            pl.BlockSpec((1, n_ctx, w), lambda bi: (bi, 0, 0)),
            pl.BlockSpec((1, n_ctx, w), lambda bi: (bi, 0, 1)),
            pl.BlockSpec((1, w, n_ctx), lambda bi: (bi, 0, 0)),
        ],
        out_specs=pl.BlockSpec((1, n_ctx, w), lambda bi: (bi, 0, 0)),
        out_shape=jax.ShapeDtypeStruct((b, n_ctx, w), BF16),
        compiler_params=_cparams(),
        name=name,
    )(qk_ctx, qk_ctx, vt_ctx)


def _na_bias_tables(rpb, seq):
    n_heads = rpb.shape[0]
    rows = seq // GRID_W
    tq = NA_ROWS_PER_BLOCK * GRID_W
    nj = seq // tq
    kh = min(NA_KH, rows)
    n_dc = rpb.shape[2]
    pad = GRID_W - 1 - (n_dc - 1) // 2
    rpb_pad = jnp.pad(rpb.astype(F32), ((0, 0), (0, 0), (pad, pad)), mode="edge")
    col_tab = jnp.stack([rpb_pad[:, :, GRID_W - 1 - qc:2 * GRID_W - 1 - qc] for qc in range(GRID_W)], axis=-1)
    tabs = []
    for j in (0, 1, nj - 1):
        ws = NA_ROWS_PER_BLOCK * int(np.clip(j - 1, 0, nj - 3))
        n_kr = 3 * NA_ROWS_PER_BLOCK
        kr = ws + np.arange(n_kr)
        r = NA_ROWS_PER_BLOCK * j + np.arange(NA_ROWS_PER_BLOCK)
        dr = np.clip(kr[:, None] - r[None, :] + NA_KH - 1, 0, 2 * NA_KH - 2)
        blocks = jnp.stack([jnp.stack([col_tab[:, dr[a, q]] for q in range(NA_ROWS_PER_BLOCK)], axis=1)
                            for a in range(n_kr)], axis=1)
        vals = blocks.transpose(0, 1, 3, 2, 4).reshape(n_heads, 3 * tq, tq)
        kk = np.arange(3 * tq)
        kr_f, kc = ws + kk // GRID_W, kk % GRID_W
        qq = np.arange(tq)
        r_f, cq = NA_ROWS_PER_BLOCK * j + qq // GRID_W, qq % GRID_W
        rs = np.clip(r_f - kh // 2, 0, rows - kh)
        cs = np.clip(cq - NA_KW // 2, 0, GRID_W - NA_KW)
        ok = ((kr_f[:, None] >= rs[None, :]) & (kr_f[:, None] < rs[None, :] + kh)
              & (kc[:, None] >= cs[None, :]) & (kc[:, None] < cs[None, :] + NA_KW))
        tabs.append(jnp.where(ok[None], vals, -jnp.inf))
    tab = jnp.stack(tabs)
    tab = tab.reshape(3, n_heads // 2, 2, 3 * tq, tq).transpose(0, 1, 3, 2, 4)
    return tab.reshape(3, n_heads // 2, 3 * tq, 2 * tq)


def _outproj_kernel(*refs, n_att):
    att_refs = refs[:n_att]
    x_ref, mod_ref, w_ref, lng_ref, lnb_ref, o_ref = refs[n_att:]
    y = None
    k0 = 0
    for a_ref in att_refs:
        kw = a_ref.shape[2]
        part = jnp.dot(a_ref[0], w_ref[k0:k0 + kw, :], preferred_element_type=F32)
        y = part if y is None else y + part
        k0 += kw
    z = ALPHA * x_ref[0] + mod_ref[0][2:3] * y
    o_ref[0] = _layer_norm(z, lng_ref[...], lnb_ref[...])


def _outproj_call(atts, x, mod, mod_row, w_o, ln_g, ln_b, tm, name):
    b, t, d = x.shape
    mod_map = (lambda bi, i: (bi, 0, 0)) if mod_row is None else (lambda bi, i: (mod_row, 0, 0))
    in_specs = [pl.BlockSpec((1, tm, a.shape[2]), lambda bi, i: (bi, i, 0)) for a in atts]
    in_specs += [
        pl.BlockSpec((1, tm, d), lambda bi, i: (bi, i, 0)),
        pl.BlockSpec((1,) + mod.shape[1:], mod_map),
        _const_spec(w_o.shape),
        _const_spec((1, d)),
        _const_spec((1, d)),
    ]
    return pl.pallas_call(
        functools.partial(_outproj_kernel, n_att=len(atts)),
        grid=(b, t // tm),
        in_specs=in_specs,
        out_specs=pl.BlockSpec((1, tm, d), lambda bi, i: (bi, i, 0)),
        out_shape=jax.ShapeDtypeStruct((b, t, d), F32),
        compiler_params=_cparams(),
        name=name,
    )(*atts, x, mod, w_o, ln_g.reshape(1, d), ln_b.reshape(1, d))


def _ffn_kernel(*refs, tm, halo):
    if halo:
        (x_ref, xp_ref, xn_ref, mod_ref, wup_ref, cw_ref, cb_ref, wdn_ref, lng_ref, lnb_ref,
         o_ref, hs_ref, act_ref) = refs
    else:
        x_ref, mod_ref, wup_ref, cw_ref, cb_ref, wdn_ref, lng_ref, lnb_ref, o_ref, hs_ref, act_ref = refs
    d = x_ref.shape[2]
    d_ff = wdn_ref.shape[0]
    rows = tm + HALO
    mod = mod_ref[0]
    shift, scale, gate = mod[3:4], mod[4:5], mod[5:6]
    x = x_ref[0]
    hs_ref[0:tm, :] = (x * (1.0 + scale) + shift).astype(BF16)
    zeros8 = jnp.zeros((SUBLANES_F32, d), F32)
    if halo:
        i = pl.program_id(1)
        last = pl.num_programs(1) - 1
        hp = jnp.where(i > 0, xp_ref[0] * (1.0 + scale) + shift, 0.0)
        hn = jnp.where(i < last, xn_ref[0] * (1.0 + scale) + shift, 0.0)
    else:
        hp = hn = zeros8
    hs_ref[tm:rows, :] = jnp.concatenate([hn, hp], axis=0).astype(BF16)
    hs = hs_ref[...]
    n_chunks = d_ff // FF_CHUNK

    def up_proj(cols):
        return jnp.dot(hs, wup_ref[:, cols], preferred_element_type=F32)

    def conv(up, cols):
        w = cw_ref[:, cols]
        return (pltpu.roll(up, 1, 0)[0:tm] * w[0:1] + up[0:tm] * w[1:2]
                + pltpu.roll(up, rows - 1, 0)[0:tm] * w[2:3] + cb_ref[:, cols])

    ups = {}
    for t in range(n_chunks + 1):
        if t < n_chunks:
            a_cols = slice(t * FF_CHUNK, (t + 1) * FF_CHUNK)
            g_cols = slice(d_ff + t * FF_CHUNK, d_ff + (t + 1) * FF_CHUNK)
            ups[t] = (up_proj(a_cols), up_proj(g_cols), a_cols, g_cols)
        if t >= 1:
            up_a, up_g, a_cols, g_cols = ups.pop(t - 1)
            act_ref[:, a_cols] = (_gelu_tanh(conv(up_g, g_cols)) * conv(up_a, a_cols)).astype(BF16)
    rb = tm // FFN_OUT_ROW_BLOCKS
    prev = None
    for r in range(FFN_OUT_ROW_BLOCKS + 1):
        cur = None
        if r < FFN_OUT_ROW_BLOCKS:
            cur = jnp.dot(act_ref[r * rb:(r + 1) * rb, :], wdn_ref[...], preferred_element_type=F32)
        if prev is not None:
            rs = slice((r - 1) * rb, r * rb)
            z = ALPHA * x_ref[0, rs, :] + gate * prev
            o_ref[0, rs, :] = _layer_norm(z, lng_ref[...], lnb_ref[...])
        prev = cur


def _ffn_call(x, mod, mod_row, w_up, conv_w, conv_b, w_down, ln_g, ln_b, tm, name):
    b, t, d = x.shape
    halo = t > tm
    mod_map = (lambda bi, i: (bi, 0, 0)) if mod_row is None else (lambda bi, i: (mod_row, 0, 0))
    in_specs = [pl.BlockSpec((1, tm, d), lambda bi, i: (bi, i, 0))]
    args = [x]
    if halo:
        r8 = tm // SUBLANES_F32
        n8 = t // SUBLANES_F32
        in_specs += [
            pl.BlockSpec((1, SUBLANES_F32, d), lambda bi, i: (bi, jnp.maximum(i * r8 - 1, 0), 0)),
            pl.BlockSpec((1, SUBLANES_F32, d), lambda bi, i: (bi, jnp.minimum((i + 1) * r8, n8 - 1), 0)),
        ]
        args += [x, x]
    in_specs += [
        pl.BlockSpec((1,) + mod.shape[1:], mod_map),
        _const_spec(w_up.shape),
        _const_spec(conv_w.shape),
        _const_spec(conv_b.shape),
        _const_spec(w_down.shape),
        _const_spec((1, d)),
        _const_spec((1, d)),
    ]
    args += [mod, w_up, conv_w, conv_b, w_down, ln_g.reshape(1, d), ln_b.reshape(1, d)]
    return pl.pallas_call(
        functools.partial(_ffn_kernel, tm=tm, halo=halo),
        grid=(b, t // tm),
        in_specs=in_specs,
        out_specs=pl.BlockSpec((1, tm, d), lambda bi, i: (bi, i, 0)),
        out_shape=jax.ShapeDtypeStruct((b, t, d), F32),
        scratch_shapes=[pltpu.VMEM((tm + HALO, d), BF16), pltpu.VMEM((tm, w_down.shape[0]), BF16)],
        compiler_params=_cparams(),
        name=name,
    )(*args)


def _rope_tables(seq):
    n_freq = HEAD_DIM // 4
    inv = ROPE_BASE ** (-jnp.arange(n_freq, dtype=F32) / n_freq)
    t = jnp.arange(seq)
    row = (t // GRID_W).astype(F32)
    col = (t % GRID_W).astype(F32)
    ar = row[:, None] * inv
    ac = col[:, None] * inv
    ang = jnp.concatenate([ar, ar, ac, ac], -1)
    cos, sin = jnp.cos(ang), jnp.sin(ang)
    even_chunk = ((np.arange(HEAD_DIM) // n_freq) % 2 == 0)[None, :]
    sin_a = jnp.where(even_chunk, -sin, 0.0)
    sin_b = jnp.where(even_chunk, 0.0, sin)
    return tuple(jnp.concatenate([a, a], axis=1) for a in (cos, sin_a, sin_b))


def _dup_heads(w, n_heads):
    d = w.shape[0]
    w = w.reshape(d, n_heads, HEAD_DIM)
    return jnp.concatenate([w, w], axis=-1).reshape(d, n_heads * PAIR)


def kernel(x, c, ctx, c_ctx, w_ada, b_ada, ln_g, ln_b, w_in_ab, w_o_ab, na_rpb, diff_lambda, diff_subln,
           w_in_c, w_o_c, gqa_qk_norm, w_up, conv_w, conv_b, w_down):
    b, s, d = x.shape
    n_ctx = ctx.shape[1]
    d_ff = w_down.shape[1]
    na_w = na_rpb.shape[1] * HEAD_DIM
    diff_w = d - na_w
    gqa_kw = GQA_KV_HEADS * HEAD_DIM
    tm_lat = min(TM_LATENT, s)

    mod_rows = -(-(b + 1) // SUBLANES_F32) * SUBLANES_F32
    ctx_row = b
    cc = jnp.concatenate([c, c_ctx[None, :], jnp.zeros((mod_rows - b - 1, d), F32)], axis=0)
    mod_all = _ada_call(cc, w_ada, b_ada)
    rope_tabs = _rope_tables(s)
    ones_gain = jnp.ones((2, LANES), F32)

    xc = ctx
    for l in range(DEPTH):
        need_ctx = l < DEPTH - 1
        i = l // 2
        mod = mod_all[l].reshape(mod_rows, 6, d)
        if l % 2 == 0:
            lambda_init = 0.8 - 0.6 * math.exp(-0.3 * l)
            w = w_in_ab[i]
            o0 = 3 * na_w
            wqk = jnp.concatenate([w[:, :2 * na_w], w[:, o0:o0 + 2 * diff_w]], axis=1).astype(BF16)
            wvt = jnp.concatenate([w[:, 2 * na_w:o0], w[:, o0 + 2 * diff_w:]], axis=1).T.astype(BF16)
            nb = na_w // LANES
            plan = ([(None, False, Q_SCALE)] * nb + [(None, False, 1.0)] * nb
                    + [(None, True, Q_SCALE)] * (diff_w // LANES) + [(None, True, 1.0)] * (diff_w // LANES))
            gains = ones_gain
        else:
            w = w_in_c[i]
            q_w = d
            wqk = jnp.concatenate([w[:, :q_w], _dup_heads(w[:, q_w:q_w + gqa_kw], GQA_KV_HEADS)],
                                  axis=1).astype(BF16)
            wvt = w[:, q_w + gqa_kw:].T.astype(BF16)
            plan = [(0, True, Q_SCALE)] * (q_w // LANES) + [(1, True, 1.0)] * GQA_KV_HEADS
            gains = jnp.concatenate([gqa_qk_norm[i], gqa_qk_norm[i]], axis=1)
        plan = tuple(plan)

        qk_lat, vt_lat = _inproj_call(x, mod, None, wqk, wvt, gains, plan, rope_tabs, tm_lat, f"inproj_lat_{l}")
        qk_ctx, vt_ctx = _inproj_call(xc, mod, ctx_row, wqk, wvt, gains, plan, None, n_ctx, f"inproj_ctx_{l}")

        if l % 2 == 0:
            bias_tab = _na_bias_tables(na_rpb[i], s) * LOG2E
            atts = [
                _na_call(qk_lat, vt_lat, qk_ctx, vt_ctx, bias_tab, f"na_lat_{l}"),
                _diff_call(qk_lat, qk_ctx, vt_ctx, qk_lat, vt_lat, diff_lambda[i], diff_subln[i], lambda_init,
                           min(TQ_DIFF, s), 1, f"diff_lat_{l}"),
            ]
            w_o = w_o_ab[i].astype(BF16)
            if need_ctx:
                atts_c = [
                    _na_ctx_call(qk_ctx, vt_ctx, na_w // LANES, f"na_ctx_{l}"),
                    _diff_call(qk_ctx, qk_ctx, vt_ctx, None, None, diff_lambda[i], diff_subln[i], lambda_init,
                               n_ctx, diff_w // LANES, f"diff_ctx_{l}"),
                ]
        else:
            atts = [_gqa_call(qk_lat, qk_ctx, vt_ctx, qk_lat, vt_lat, min(TQ_GQA, s), 1, f"gqa_lat_{l}")]
            w_o = w_o_c[i].astype(BF16)
            if need_ctx:
                atts_c = [_gqa_call(qk_ctx, qk_ctx, vt_ctx, None, None, n_ctx, GQA_KV_HEADS, f"gqa_ctx_{l}")]

        w_up_l = w_up[l].astype(BF16)
        conv_w_l = conv_w[l]
        conv_b_l = conv_b[l].reshape(1, 2 * d_ff)
        w_down_l = w_down[l].astype(BF16)

        x = _outproj_call(atts, x, mod, None, w_o, ln_g[l, 0], ln_b[l, 0], tm_lat, f"outproj_lat_{l}")
        x = _ffn_call(x, mod, None, w_up_l, conv_w_l, conv_b_l, w_down_l, ln_g[l, 1], ln_b[l, 1],
                      tm_lat, f"ffn_lat_{l}")
        if need_ctx:
            xc = _outproj_call(atts_c, xc, mod, ctx_row, w_o, ln_g[l, 0], ln_b[l, 0], n_ctx, f"outproj_ctx_{l}")
            xc = _ffn_call(xc, mod, ctx_row, w_up_l, conv_w_l, conv_b_l, w_down_l, ln_g[l, 1], ln_b[l, 1],
                           n_ctx, f"ffn_ctx_{l}")
    return x
```

```python
import functools
import math

import numpy as np
import jax
import jax.numpy as jnp
from jax import lax
from jax.experimental import pallas as pl
from jax.experimental.pallas import tpu as pltpu

F32 = jnp.float32
BF16 = jnp.bfloat16

DEPTH = 4
GRID_W = 64
HEAD_DIM = 64
NA_KH = 8
NA_KW = 16
GQA_KV_HEADS = 4
CONV_WIDTH = 3
ROPE_BASE = 10000.0
EPS = 1e-6
ALPHA = (2 * DEPTH) ** 0.25
LOG2E = math.log2(math.e)
Q_SCALE = HEAD_DIM ** -0.5 * LOG2E

LANES = 128
SUBLANES_F32 = 8
SUBLANES_BF16 = 16
MXU_DIM = 256
VMEM_LIMIT_BYTES = 56 * 1024 * 1024

PAIR = 2 * HEAD_DIM
assert PAIR == LANES

TM_LATENT = 512
TQ_GQA = 1024
TQ_DIFF = 2048
NA_ROWS_PER_BLOCK = 4
KEY_CHUNK = MXU_DIM
ONES_ROWS = SUBLANES_BF16
FF_CHUNK = MXU_DIM
ADA_TN = 1536
INPROJ_GROUP = MXU_DIM
HALO = SUBLANES_BF16
FFN_OUT_ROW_BLOCKS = 2


def _cparams():
    return pltpu.CompilerParams(vmem_limit_bytes=VMEM_LIMIT_BYTES)


def _const_spec(shape):
    nd = len(shape)
    return pl.BlockSpec(shape, lambda *_: (0,) * nd, pipeline_mode=pl.Buffered(1))


def _layer_norm(z, g, b):
    mu = jnp.mean(z, axis=-1, keepdims=True)
    d = z - mu
    var = jnp.mean(d * d, axis=-1, keepdims=True)
    return d * lax.rsqrt(var + EPS) * g + b


def _gelu_tanh(x):
    c = math.sqrt(2.0 / math.pi)
    return (0.5 * x) * (1.0 + jnp.tanh(x * (c + (0.044715 * c) * (x * x))))


def _group_sum_matrix():
    r = lax.broadcasted_iota(jnp.int32, (2 * LANES, LANES), 0)
    c = lax.broadcasted_iota(jnp.int32, (2 * LANES, LANES), 1)
    same = ((r % LANES) // HEAD_DIM) == (c // HEAD_DIM)
    return jnp.where(same, 1.0, 0.0).astype(BF16)


def _head_rms(xb, gm, g):
    x2 = xb * xb
    hi = x2.astype(BF16)
    lo = (x2 - hi.astype(F32)).astype(BF16)
    ssq = jnp.dot(jnp.concatenate([hi, lo], axis=1), gm, preferred_element_type=F32)
    return xb * lax.rsqrt(ssq * (1.0 / HEAD_DIM) + EPS) * g


def _rope(xb, cos, sin_a, sin_b):
    return (xb * cos
            + pltpu.roll(xb, LANES - HEAD_DIM // 4, 1) * sin_a
            + pltpu.roll(xb, HEAD_DIM // 4, 1) * sin_b)


def _split_heads_rows(q_pair):
    qf = q_pair.astype(F32)
    lane = lax.broadcasted_iota(jnp.int32, qf.shape, 1)
    lo = lane < HEAD_DIM
    return [jnp.where(lo, qf, 0.0).astype(BF16), jnp.where(lo, 0.0, qf).astype(BF16)]


def _attend(streams):
    n_chunks = len(streams[0][1])
    s_buf, e_buf = {}, {}
    m = [None] * len(streams)
    acc = [None] * len(streams)
    for t in range(n_chunks + 2):
        for i, (q_stack, chunks) in enumerate(streams):
            if t < n_chunks:
                k_c, _, bias_c = chunks[t]
                s = lax.dot_general(k_c, q_stack, (((1,), (1,)), ((), ())), preferred_element_type=F32)
                s_buf[i, t] = s if bias_c is None else s + bias_c
        for i, (q_stack, chunks) in enumerate(streams):
            if 0 <= t - 1 < n_chunks:
                s = s_buf.pop((i, t - 1))
                mc = jnp.max(s, axis=0, keepdims=True)
                m_new = mc if m[i] is None else jnp.maximum(m[i], mc)
                alpha = None if m[i] is None else jnp.exp2(m[i] - m_new)
                m[i] = m_new
                e_buf[i, t - 1] = (jnp.exp2(s - m_new).astype(BF16), alpha)
        for i, (q_stack, chunks) in enumerate(streams):
            if 0 <= t - 2 < n_chunks:
                e, alpha = e_buf.pop((i, t - 2))
                vt_c = chunks[t - 2][1]
                vt_aug = jnp.concatenate([vt_c, jnp.ones((ONES_ROWS, vt_c.shape[1]), BF16)], axis=0)
                pv = jnp.dot(vt_aug, e, preferred_element_type=F32)
                acc[i] = pv if alpha is None else alpha * acc[i] + pv
    outs = []
    for (q_stack, chunks), a in zip(streams, acc):
        dv = chunks[0][1].shape[0]
        outs.append(a[:dv] * (1.0 / a[dv:dv + 1]))
    return outs


def _key_chunks(kc_ref, vc_ref, kl_ref, vl_ref, k_cols=slice(None), v_rows=slice(None)):
    chunks = [(kc_ref[0, :, k_cols], vc_ref[0, v_rows, :], None)]
    if kl_ref is not None:
        n_lat = kl_ref.shape[1]
        for c0 in range(0, n_lat, KEY_CHUNK):
            chunks.append((kl_ref[0, c0:c0 + KEY_CHUNK, k_cols], vl_ref[0, v_rows, c0:c0 + KEY_CHUNK], None))
    return chunks


def _pick_head_rows(res, tq, p):
    row = lax.broadcasted_iota(jnp.int32, (LANES, tq), 0)
    a = res[:, (2 * p) * tq:(2 * p + 1) * tq]
    b = res[:, (2 * p + 1) * tq:(2 * p + 2) * tq]
    return jnp.where(row < HEAD_DIM, a, b).T


def _stack_head_rows(res, tq, p):
    return jnp.concatenate([res[:, (2 * p) * tq:(2 * p + 1) * tq],
                            res[:, (2 * p + 1) * tq:(2 * p + 2) * tq]], axis=0).T


def _ada_kernel(c_ref, w_ref, b_ref, o_ref):
    c = c_ref[...]
    cond = c * (1.0 / (1.0 + jnp.exp(-c)))
    o_ref[0] = jnp.dot(cond, w_ref[0], precision=lax.Precision.HIGHEST,
                       preferred_element_type=F32) + b_ref[0]


def _ada_call(cc, w_ada, b_ada):
    n_layers, d, n_out = w_ada.shape
    rows = cc.shape[0]
    return pl.pallas_call(
        _ada_kernel,
        grid=(n_layers, n_out // ADA_TN),
        in_specs=[
            pl.BlockSpec((rows, d), lambda l, n: (0, 0)),
            pl.BlockSpec((1, d, ADA_TN), lambda l, n: (l, 0, n)),
            pl.BlockSpec((1, 1, ADA_TN), lambda l, n: (l, 0, n)),
        ],
        out_specs=pl.BlockSpec((1, rows, ADA_TN), lambda l, n: (l, 0, n)),
        out_shape=jax.ShapeDtypeStruct((n_layers, rows, n_out), F32),
        compiler_params=_cparams(),
        name="adaln_mod",
    )(cc, w_ada, b_ada.reshape(n_layers, 1, n_out))


def _inproj_kernel(*refs, block_plan, rope):
    if rope:
        x_ref, mod_ref, wqk_ref, wvt_ref, g_ref, cos_ref, sa_ref, sb_ref, oqk_ref, ovt_ref = refs
    else:
        x_ref, mod_ref, wqk_ref, wvt_ref, g_ref, oqk_ref, ovt_ref = refs
    mod = mod_ref[0]
    h = (x_ref[0] * (1.0 + mod[1:2]) + mod[0:1]).astype(BF16)
    has_norm = any(p[0] is not None for p in block_plan)
    gm = _group_sum_matrix() if has_norm else None
    if rope:
        cos, sa, sb = cos_ref[...], sa_ref[...], sb_ref[...]

    def v_proj():
        vt = lax.dot_general(wvt_ref[...], h, (((1,), (1,)), ((), ())), preferred_element_type=F32)
        ovt_ref[0] = vt.astype(BF16)

    def epilogue(j0, qk):
        for jj in range(qk.shape[1] // LANES):
            gain_row, do_rope, scale = block_plan[j0 + jj]
            blk = qk[:, jj * LANES:(jj + 1) * LANES]
            if gain_row is not None:
                blk = _head_rms(blk, gm, g_ref[gain_row:gain_row + 1, :])
            if do_rope and rope:
                blk = _rope(blk, cos, sa, sb)
            if scale != 1.0:
                blk = blk * scale
            oqk_ref[0, :, (j0 + jj) * LANES:(j0 + jj + 1) * LANES] = blk.astype(BF16)

    if not has_norm:
        qk = jnp.dot(h, wqk_ref[...], preferred_element_type=F32)
        v_proj()
        epilogue(0, qk)
        return
    bpg = INPROJ_GROUP // LANES
    n_groups = len(block_plan) // bpg
    prev = None
    for g in range(n_groups + 1):
        cur = None
        if g < n_groups:
            cur = jnp.dot(h, wqk_ref[:, g * INPROJ_GROUP:(g + 1) * INPROJ_GROUP], preferred_element_type=F32)
        else:
            v_proj()
        if prev is not None:
            epilogue((g - 1) * bpg, prev)
        prev = cur


def _inproj_call(x, mod, mod_row, wqk, wvt, gains, block_plan, rope_tabs, tm, name):
    b, t, d = x.shape
    wq = wqk.shape[1]
    wv = wvt.shape[0]
    rope = rope_tabs is not None
    mod_map = (lambda bi, i: (bi, 0, 0)) if mod_row is None else (lambda bi, i: (mod_row, 0, 0))
    in_specs = [
        pl.BlockSpec((1, tm, d), lambda bi, i: (bi, i, 0)),
        pl.BlockSpec((1,) + mod.shape[1:], mod_map),
        _const_spec(wqk.shape),
        _const_spec(wvt.shape),
        _const_spec(gains.shape),
    ]
    args = [x, mod, wqk, wvt, gains]
    if rope:
        in_specs += [pl.BlockSpec((tm, LANES), lambda bi, i: (i, 0))] * 3
        args += list(rope_tabs)
    return pl.pallas_call(
        functools.partial(_inproj_kernel, block_plan=block_plan, rope=rope),
        grid=(b, t // tm),
        in_specs=in_specs,
        out_specs=[
            pl.BlockSpec((1, tm, wq), lambda bi, i: (bi, i, 0)),
            pl.BlockSpec((1, wv, tm), lambda bi, i: (bi, 0, i)),
        ],
        out_shape=[jax.ShapeDtypeStruct((b, t, wq), BF16), jax.ShapeDtypeStruct((b, wv, t), BF16)],
        compiler_params=_cparams(),
        name=name,
    )(*args)


def _gqa_kernel(*refs, tq, latent_keys):
    if latent_keys:
        q_ref, kc_ref, vc_ref, kl_ref, vl_ref, o_ref = refs
    else:
        q_ref, kc_ref, vc_ref, o_ref = refs
        kl_ref = vl_ref = None
    n_groups = q_ref.shape[2] // (2 * LANES)
    streams = []
    for g in range(n_groups):
        q = q_ref[0, :, g * 2 * LANES:(g + 1) * 2 * LANES]
        parts = _split_heads_rows(q[:, :LANES]) + _split_heads_rows(q[:, LANES:])
        chunks = _key_chunks(kc_ref, vc_ref, kl_ref, vl_ref, slice(g * LANES, (g + 1) * LANES),
                             slice(g * HEAD_DIM, (g + 1) * HEAD_DIM))
        streams.append((jnp.concatenate(parts, axis=0), chunks))
    for g, res in enumerate(_attend(streams)):
        o_ref[0, :, g * 2 * LANES:(g + 1) * 2 * LANES] = jnp.concatenate(
            [_stack_head_rows(res, tq, 0), _stack_head_rows(res, tq, 1)], axis=1).astype(BF16)


def _gqa_call(qk_q, qk_ctx, vt_ctx, qk_lat, vt_lat, tq, groups, name):
    b, t, _ = qk_q.shape
    n_ctx = qk_ctx.shape[1]
    k_blk0 = (GQA_KV_HEADS * 4 * HEAD_DIM) // (groups * LANES)
    latent_keys = qk_lat is not None
    in_specs = [
        pl.BlockSpec((1, tq, groups * 2 * LANES), lambda bi, g, i: (bi, i, g)),
        pl.BlockSpec((1, n_ctx, groups * LANES), lambda bi, g, i: (bi, 0, k_blk0 + g)),
        pl.BlockSpec((1, groups * HEAD_DIM, n_ctx), lambda bi, g, i: (bi, g, 0)),
    ]
    args = [qk_q, qk_ctx, vt_ctx]
    if latent_keys:
        n_lat = qk_lat.shape[1]
        in_specs += [
            pl.BlockSpec((1, n_lat, groups * LANES), lambda bi, g, i: (bi, 0, k_blk0 + g)),
            pl.BlockSpec((1, groups * HEAD_DIM, n_lat), lambda bi, g, i: (bi, g, 0)),
        ]
        args += [qk_lat, vt_lat]
    return pl.pallas_call(
        functools.partial(_gqa_kernel, tq=tq, latent_keys=latent_keys),
        grid=(b, GQA_KV_HEADS // groups, t // tq),
        in_specs=in_specs,
        out_specs=pl.BlockSpec((1, tq, groups * 2 * LANES), lambda bi, g, i: (bi, i, g)),
        out_shape=jax.ShapeDtypeStruct((b, t, GQA_KV_HEADS * 2 * LANES), BF16),
        compiler_params=_cparams(),
        name=name,
    )(*args)


def _diff_kernel(*refs, tq, latent_keys, lambda_init):
    if latent_keys:
        q_ref, kc_ref, vc_ref, kl_ref, vl_ref, lam_ref, g_ref, o_ref = refs
    else:
        q_ref, kc_ref, vc_ref, lam_ref, g_ref, o_ref = refs
        kl_ref = vl_ref = None
    n_heads = q_ref.shape[2] // LANES
    streams = []
    for hd in range(n_heads):
        cols = slice(hd * LANES, (hd + 1) * LANES)
        q_stack = jnp.concatenate(_split_heads_rows(q_ref[0, :, cols]), axis=0)
        streams.append((q_stack, _key_chunks(kc_ref, vc_ref, kl_ref, vl_ref, cols, cols)))
    lv = lam_ref[...]
    lam = (jnp.exp(jnp.sum(lv[0:1] * lv[1:2], axis=1, keepdims=True))
           - jnp.exp(jnp.sum(lv[2:3] * lv[3:4], axis=1, keepdims=True)) + lambda_init)
    for hd, res in enumerate(_attend(streams)):
        o = (res[:, :tq] - lam * res[:, tq:]).T
        o = o * lax.rsqrt(jnp.mean(o * o, axis=-1, keepdims=True) + EPS) * g_ref[...]
        o_ref[0, :, hd * LANES:(hd + 1) * LANES] = (o * (1.0 - lambda_init)).astype(BF16)


def _diff_call(qk_q, qk_ctx, vt_ctx, qk_lat, vt_lat, lam_vec, subln_g, lambda_init, tq, heads, name):
    b, t, wqk = qk_q.shape
    n_ctx = qk_ctx.shape[1]
    n_heads = wqk // (4 * LANES)
    n_steps = n_heads // heads
    q_blk0, k_blk0, v_blk0 = 2 * n_steps, 3 * n_steps, n_steps
    w = heads * LANES
    latent_keys = qk_lat is not None
    in_specs = [
        pl.BlockSpec((1, tq, w), lambda bi, h, i: (bi, i, q_blk0 + h)),
        pl.BlockSpec((1, n_ctx, w), lambda bi, h, i: (bi, 0, k_blk0 + h)),
        pl.BlockSpec((1, w, n_ctx), lambda bi, h, i: (bi, v_blk0 + h, 0)),
    ]
    args = [qk_q, qk_ctx, vt_ctx]
    if latent_keys:
        n_lat = qk_lat.shape[1]
        in_specs += [
            pl.BlockSpec((1, n_lat, w), lambda bi, h, i: (bi, 0, k_blk0 + h)),
            pl.BlockSpec((1, w, n_lat), lambda bi, h, i: (bi, v_blk0 + h, 0)),
        ]
        args += [qk_lat, vt_lat]
    in_specs += [_const_spec(lam_vec.shape), _const_spec((1, LANES))]
    args += [lam_vec, subln_g.reshape(1, LANES)]
    return pl.pallas_call(
        functools.partial(_diff_kernel, tq=tq, latent_keys=latent_keys, lambda_init=lambda_init),
        grid=(b, n_steps, t // tq),
        in_specs=in_specs,
        out_specs=pl.BlockSpec((1, tq, w), lambda bi, h, i: (bi, i, h)),
        out_shape=jax.ShapeDtypeStruct((b, t, n_heads * LANES), BF16),
        compiler_params=_cparams(),
        name=name,
    )(*args)


def _na_kernel(*refs, n_local):
    q_ref, kc_ref, vc_ref = refs[:3]
    k_refs = refs[3:3 + n_local]
    v_refs = refs[3 + n_local:3 + 2 * n_local]
    bias_ref = refs[3 + 2 * n_local] if n_local else None
    o_ref = refs[-1]
    tq = q_ref.shape[1]
    n_pairs = q_ref.shape[2] // LANES
    streams = []
    for p in range(n_pairs):
        cols = slice(p * LANES, (p + 1) * LANES)
        q_stack = jnp.concatenate(_split_heads_rows(q_ref[0, :, cols]), axis=0)
        chunks = [(kc_ref[0, :, cols], vc_ref[0, cols, :], None)]
        for c in range(n_local):
            chunks.append((k_refs[c][0, :, cols], v_refs[c][0, cols, :], bias_ref[0, p, c * tq:(c + 1) * tq, :]))
        streams.append((q_stack, chunks))
    for p, res in enumerate(_attend(streams)):
        o_ref[0, :, p * LANES:(p + 1) * LANES] = _pick_head_rows(res, tq, 0).astype(BF16)


def _na_call(qk_lat, vt_lat, qk_ctx, vt_ctx, bias_tab, name):
    b, s, _ = qk_lat.shape
    n_ctx = qk_ctx.shape[1]
    tq = NA_ROWS_PER_BLOCK * GRID_W
    nj = s // tq
    n_pairs = bias_tab.shape[1]
    w = n_pairs * LANES
    n_local = 3

    def tb(j):
        return jnp.clip(j - 1, 0, nj - n_local)

    def cls(j):
        return jnp.where(j == 0, 0, jnp.where(j == nj - 1, 2, 1))

    k_specs = [pl.BlockSpec((1, tq, w), functools.partial(lambda j, bi, d: (bi, tb(j) + d, 1), d=d))
               for d in range(n_local)]
    v_specs = [pl.BlockSpec((1, w, tq), functools.partial(lambda j, bi, d: (bi, 0, tb(j) + d), d=d))
               for d in range(n_local)]
    return pl.pallas_call(
        functools.partial(_na_kernel, n_local=n_local),
        grid=(nj, b),
        in_specs=[
            pl.BlockSpec((1, tq, w), lambda j, bi: (bi, j, 0)),
            pl.BlockSpec((1, n_ctx, w), lambda j, bi: (bi, 0, 1)),
            pl.BlockSpec((1, w, n_ctx), lambda j, bi: (bi, 0, 0)),
            *k_specs, *v_specs,
            pl.BlockSpec((1, n_pairs, n_local * tq, 2 * tq), lambda j, bi: (cls(j), 0, 0, 0)),
        ],
        out_specs=pl.BlockSpec((1, tq, w), lambda j, bi: (bi, j, 0)),
        out_shape=jax.ShapeDtypeStruct((b, s, w), BF16),
        compiler_params=_cparams(),
        name=name,
    )(qk_lat, qk_ctx, vt_ctx, *([qk_lat] * n_local), *([vt_lat] * n_local), bias_tab)


def _na_ctx_call(qk_ctx, vt_ctx, n_pairs, name):
    b, n_ctx, _ = qk_ctx.shape
    w = n_pairs * LANES
    return pl.pallas_call(
        functools.partial(_na_kernel, n_local=0),
        grid=(b,),
        in_specs=[
            pl.BlockSpec((1, n_ctx, w), lambda bi: (bi, 0, 0)),
            pl.BlockSpec((1, n_ctx, w), lambda bi: (bi, 0, 1)),
            pl.BlockSpec((1, w, n_ctx), lambda bi: (bi, 0, 0)),
        ],
        out_specs=pl.BlockSpec((1, n_ctx, w), lambda bi: (bi, 0, 0)),
        out_shape=jax.ShapeDtypeStruct((b, n_ctx, w), BF16),
        compiler_params=_cparams(),
        name=name,
    )(qk_ctx, qk_ctx, vt_ctx)


def _na_bias_tables(rpb, seq):
    n_heads = rpb.shape[0]
    rows = seq // GRID_W
    tq = NA_ROWS_PER_BLOCK * GRID_W
    nj = seq // tq
    kh = min(NA_KH, rows)
    n_dc = rpb.shape[2]
    n_pairs = n_heads // 2
    pad = GRID_W - 1 - (n_dc - 1) // 2
    rpb_pad = jnp.pad(rpb.astype(F32), ((0, 0), (0, 0), (pad, pad)), mode="edge")
    col_tab = jnp.stack([rpb_pad[:, :, GRID_W - 1 - qc:2 * GRID_W - 1 - qc] for qc in range(GRID_W)], axis=-1)
    cols = np.arange(GRID_W)
    cs = np.clip(cols - NA_KW // 2, 0, GRID_W - NA_KW)
    col_ok = (cols[:, None] >= cs[None, :]) & (cols[:, None] < cs[None, :] + NA_KW)
    col_tab = jnp.where(col_ok[None, None], col_tab, -jnp.inf)
    col_tab = col_tab.reshape(n_pairs, 2, col_tab.shape[1], GRID_W, GRID_W).transpose(0, 2, 3, 1, 4)
    masked = jnp.full((n_pairs, GRID_W, 2, GRID_W), -jnp.inf, F32)
    n_kr = 3 * NA_ROWS_PER_BLOCK
    tabs = []
    for j in (0, 1, nj - 1):
        ws = NA_ROWS_PER_BLOCK * int(np.clip(j - 1, 0, nj - 3))
        key_rows = []
        for a in range(n_kr):
            blocks = []
            for qr in range(NA_ROWS_PER_BLOCK):
                kr, r = ws + a, NA_ROWS_PER_BLOCK * j + qr
                rs = int(np.clip(r - kh // 2, 0, rows - kh))
                row_ok = rs <= kr < rs + kh
                blocks.append(col_tab[:, kr - r + NA_KH - 1] if row_ok else masked)
            key_rows.append(jnp.stack(blocks, axis=3))
        tabs.append(jnp.stack(key_rows, axis=1).reshape(n_pairs, 3 * tq, 2 * tq))
    return jnp.stack(tabs)


def _outproj_kernel(*refs, n_att):
    att_refs = refs[:n_att]
    x_ref, mod_ref, w_ref, lng_ref, lnb_ref, o_ref = refs[n_att:]
    y = None
    k0 = 0
    for a_ref in att_refs:
        kw = a_ref.shape[2]
        part = jnp.dot(a_ref[0], w_ref[k0:k0 + kw, :], preferred_element_type=F32)
        y = part if y is None else y + part
        k0 += kw
    z = ALPHA * x_ref[0] + mod_ref[0][2:3] * y
    o_ref[0] = _layer_norm(z, lng_ref[...], lnb_ref[...])


def _outproj_call(atts, x, mod, mod_row, w_o, ln_g, ln_b, tm, name):
    b, t, d = x.shape
    mod_map = (lambda bi, i: (bi, 0, 0)) if mod_row is None else (lambda bi, i: (mod_row, 0, 0))
    in_specs = [pl.BlockSpec((1, tm, a.shape[2]), lambda bi, i: (bi, i, 0)) for a in atts]
    in_specs += [
        pl.BlockSpec((1, tm, d), lambda bi, i: (bi, i, 0)),
        pl.BlockSpec((1,) + mod.shape[1:], mod_map),
        _const_spec(w_o.shape),
        _const_spec((1, d)),
        _const_spec((1, d)),
    ]
    return pl.pallas_call(
        functools.partial(_outproj_kernel, n_att=len(atts)),
        grid=(b, t // tm),
        in_specs=in_specs,
        out_specs=pl.BlockSpec((1, tm, d), lambda bi, i: (bi, i, 0)),
        out_shape=jax.ShapeDtypeStruct((b, t, d), F32),
        compiler_params=_cparams(),
        name=name,
    )(*atts, x, mod, w_o, ln_g.reshape(1, d), ln_b.reshape(1, d))


def _ffn_kernel(*refs, tm, halo):
    if halo:
        (x_ref, xp_ref, xn_ref, mod_ref, wup_ref, cw_ref, cb_ref, wdn_ref, lng_ref, lnb_ref,
         o_ref, hs_ref, act_ref) = refs
    else:
        x_ref, mod_ref, wup_ref, cw_ref, cb_ref, wdn_ref, lng_ref, lnb_ref, o_ref, hs_ref, act_ref = refs
    d = x_ref.shape[2]
    d_ff = wdn_ref.shape[0]
    rows = tm + HALO
    mod = mod_ref[0]
    shift, scale, gate = mod[3:4], mod[4:5], mod[5:6]
    x = x_ref[0]
    hs_ref[0:tm, :] = (x * (1.0 + scale) + shift).astype(BF16)
    zeros8 = jnp.zeros((SUBLANES_F32, d), F32)
    if halo:
        i = pl.program_id(1)
        last = pl.num_programs(1) - 1
        hp = jnp.where(i > 0, xp_ref[0] * (1.0 + scale) + shift, 0.0)
        hn = jnp.where(i < last, xn_ref[0] * (1.0 + scale) + shift, 0.0)
    else:
        hp = hn = zeros8
    hs_ref[tm:rows, :] = jnp.concatenate([hn, hp], axis=0).astype(BF16)
    hs = hs_ref[...]
    n_chunks = d_ff // FF_CHUNK

    def up_proj(cols):
        return jnp.dot(hs, wup_ref[:, cols], preferred_element_type=F32)

    def conv(up, cols):
        w = cw_ref[:, cols]
        return (pltpu.roll(up, 1, 0)[0:tm] * w[0:1] + up[0:tm] * w[1:2]
                + pltpu.roll(up, rows - 1, 0)[0:tm] * w[2:3] + cb_ref[:, cols])

    ups = {}
    for t in range(n_chunks + 1):
        if t < n_chunks:
            a_cols = slice(t * FF_CHUNK, (t + 1) * FF_CHUNK)
            g_cols = slice(d_ff + t * FF_CHUNK, d_ff + (t + 1) * FF_CHUNK)
            ups[t] = (up_proj(a_cols), up_proj(g_cols), a_cols, g_cols)
        if t >= 1:
            up_a, up_g, a_cols, g_cols = ups.pop(t - 1)
            act_ref[:, a_cols] = (_gelu_tanh(conv(up_g, g_cols)) * conv(up_a, a_cols)).astype(BF16)
    rb = tm // FFN_OUT_ROW_BLOCKS
    prev = None
    for r in range(FFN_OUT_ROW_BLOCKS + 1):
        cur = None
        if r < FFN_OUT_ROW_BLOCKS:
            cur = jnp.dot(act_ref[r * rb:(r + 1) * rb, :], wdn_ref[...], preferred_element_type=F32)
        if prev is not None:
            rs = slice((r - 1) * rb, r * rb)
            z = ALPHA * x_ref[0, rs, :] + gate * prev
            o_ref[0, rs, :] = _layer_norm(z, lng_ref[...], lnb_ref[...])
        prev = cur


def _ffn_call(x, mod, mod_row, w_up, conv_w, conv_b, w_down, ln_g, ln_b, tm, name):
    b, t, d = x.shape
    halo = t > tm
    mod_map = (lambda bi, i: (bi, 0, 0)) if mod_row is None else (lambda bi, i: (mod_row, 0, 0))
    in_specs = [pl.BlockSpec((1, tm, d), lambda bi, i: (bi, i, 0))]
    args = [x]
    if halo:
        r8 = tm // SUBLANES_F32
        n8 = t // SUBLANES_F32
        in_specs += [
            pl.BlockSpec((1, SUBLANES_F32, d), lambda bi, i: (bi, jnp.maximum(i * r8 - 1, 0), 0)),
            pl.BlockSpec((1, SUBLANES_F32, d), lambda bi, i: (bi, jnp.minimum((i + 1) * r8, n8 - 1), 0)),
        ]
        args += [x, x]
    in_specs += [
        pl.BlockSpec((1,) + mod.shape[1:], mod_map),
        _const_spec(w_up.shape),
        _const_spec(conv_w.shape),
        _const_spec(conv_b.shape),
        _const_spec(w_down.shape),
        _const_spec((1, d)),
        _const_spec((1, d)),
    ]
    args += [mod, w_up, conv_w, conv_b, w_down, ln_g.reshape(1, d), ln_b.reshape(1, d)]
    return pl.pallas_call(
        functools.partial(_ffn_kernel, tm=tm, halo=halo),
        grid=(b, t // tm),
        in_specs=in_specs,
        out_specs=pl.BlockSpec((1, tm, d), lambda bi, i: (bi, i, 0)),
        out_shape=jax.ShapeDtypeStruct((b, t, d), F32),
        scratch_shapes=[pltpu.VMEM((tm + HALO, d), BF16), pltpu.VMEM((tm, w_down.shape[0]), BF16)],
        compiler_params=_cparams(),
        name=name,
    )(*args)


def _rope_tables(seq):
    n_freq = HEAD_DIM // 4
    inv = ROPE_BASE ** (-jnp.arange(n_freq, dtype=F32) / n_freq)
    t = jnp.arange(seq)
    row = (t // GRID_W).astype(F32)
    col = (t % GRID_W).astype(F32)
    ar = row[:, None] * inv
    ac = col[:, None] * inv
    ang = jnp.concatenate([ar, ar, ac, ac], -1)
    cos, sin = jnp.cos(ang), jnp.sin(ang)
    even_chunk = ((np.arange(HEAD_DIM) // n_freq) % 2 == 0)[None, :]
    sin_a = jnp.where(even_chunk, -sin, 0.0)
    sin_b = jnp.where(even_chunk, 0.0, sin)
    return tuple(jnp.concatenate([a, a], axis=1) for a in (cos, sin_a, sin_b))


def _dup_heads(w, n_heads):
    d = w.shape[0]
    w = w.reshape(d, n_heads, HEAD_DIM)
    return jnp.concatenate([w, w], axis=-1).reshape(d, n_heads * PAIR)


def kernel(x, c, ctx, c_ctx, w_ada, b_ada, ln_g, ln_b, w_in_ab, w_o_ab, na_rpb, diff_lambda, diff_subln,
           w_in_c, w_o_c, gqa_qk_norm, w_up, conv_w, conv_b, w_down):
    b, s, d = x.shape
    n_ctx = ctx.shape[1]
    d_ff = w_down.shape[1]
    na_w = na_rpb.shape[1] * HEAD_DIM
    diff_w = d - na_w
    gqa_kw = GQA_KV_HEADS * HEAD_DIM
    tm_lat = min(TM_LATENT, s)

    mod_rows = -(-(b + 1) // SUBLANES_F32) * SUBLANES_F32
    ctx_row = b
    cc = jnp.concatenate([c, c_ctx[None, :], jnp.zeros((mod_rows - b - 1, d), F32)], axis=0)
    mod_all = _ada_call(cc, w_ada, b_ada)
    rope_tabs = _rope_tables(s)
    ones_gain = jnp.ones((2, LANES), F32)

    xc = ctx
    for l in range(DEPTH):
        need_ctx = l < DEPTH - 1
        i = l // 2
        mod = mod_all[l].reshape(mod_rows, 6, d)
        if l % 2 == 0:
            lambda_init = 0.8 - 0.6 * math.exp(-0.3 * l)
            w = w_in_ab[i]
            o0 = 3 * na_w
            wqk = jnp.concatenate([w[:, :2 * na_w], w[:, o0:o0 + 2 * diff_w]], axis=1).astype(BF16)
            wvt = jnp.concatenate([w[:, 2 * na_w:o0], w[:, o0 + 2 * diff_w:]], axis=1).T.astype(BF16)
            nb = na_w // LANES
            plan = ([(None, False, Q_SCALE)] * nb + [(None, False, 1.0)] * nb
                    + [(None, True, Q_SCALE)] * (diff_w // LANES) + [(None, True, 1.0)] * (diff_w // LANES))
            gains = ones_gain
        else:
            w = w_in_c[i]
            q_w = d
            wqk = jnp.concatenate([w[:, :q_w], _dup_heads(w[:, q_w:q_w + gqa_kw], GQA_KV_HEADS)],
                                  axis=1).astype(BF16)
            wvt = w[:, q_w + gqa_kw:].T.astype(BF16)
            plan = [(0, True, Q_SCALE)] * (q_w // LANES) + [(1, True, 1.0)] * GQA_KV_HEADS
            gains = jnp.concatenate([gqa_qk_norm[i], gqa_qk_norm[i]], axis=1)
        plan = tuple(plan)

        qk_lat, vt_lat = _inproj_call(x, mod, None, wqk, wvt, gains, plan, rope_tabs, tm_lat, f"inproj_lat_{l}")
        qk_ctx, vt_ctx = _inproj_call(xc, mod, ctx_row, wqk, wvt, gains, plan, None, n_ctx, f"inproj_ctx_{l}")

        if l % 2 == 0:
            bias_tab = _na_bias_tables(na_rpb[i] * LOG2E, s)
            atts = [
                _na_call(qk_lat, vt_lat, qk_ctx, vt_ctx, bias_tab, f"na_lat_{l}"),
                _diff_call(qk_lat, qk_ctx, vt_ctx, qk_lat, vt_lat, diff_lambda[i], diff_subln[i], lambda_init,
                           min(TQ_DIFF, s), 1, f"diff_lat_{l}"),
            ]
            w_o = w_o_ab[i].astype(BF16)
            if need_ctx:
                atts_c = [
                    _na_ctx_call(qk_ctx, vt_ctx, na_w // LANES, f"na_ctx_{l}"),
                    _diff_call(qk_ctx, qk_ctx, vt_ctx, None, None, diff_lambda[i], diff_subln[i], lambda_init,
                               n_ctx, diff_w // LANES, f"diff_ctx_{l}"),
                ]
        else:
            atts = [_gqa_call(qk_lat, qk_ctx, vt_ctx, qk_lat, vt_lat, min(TQ_GQA, s), 1, f"gqa_lat_{l}")]
            w_o = w_o_c[i].astype(BF16)
            if need_ctx:
                atts_c = [_gqa_call(qk_ctx, qk_ctx, vt_ctx, None, None, n_ctx, GQA_KV_HEADS, f"gqa_ctx_{l}")]

        w_up_l = w_up[l].astype(BF16)
        conv_w_l = conv_w[l]
        conv_b_l = conv_b[l].reshape(1, 2 * d_ff)
        w_down_l = w_down[l].astype(BF16)

        x = _outproj_call(atts, x, mod, None, w_o, ln_g[l, 0], ln_b[l, 0], tm_lat, f"outproj_lat_{l}")
        x = _ffn_call(x, mod, None, w_up_l, conv_w_l, conv_b_l, w_down_l, ln_g[l, 1], ln_b[l, 1],
                      tm_lat, f"ffn_lat_{l}")
        if need_ctx:
            xc = _outproj_call(atts_c, xc, mod, ctx_row, w_o, ln_g[l, 0], ln_b[l, 0], n_ctx, f"outproj_ctx_{l}")
            xc = _ffn_call(xc, mod, ctx_row, w_up_l, conv_w_l, conv_b_l, w_down_l, ln_g[l, 1], ln_b[l, 1],
                           n_ctx, f"ffn_ctx_{l}")
    return x
```

```python
import functools
import math

import numpy as np
import jax
import jax.numpy as jnp
from jax import lax
from jax.experimental import pallas as pl
from jax.experimental.pallas import tpu as pltpu

F32 = jnp.float32
BF16 = jnp.bfloat16

DEPTH = 4
GRID_W = 64
HEAD_DIM = 64
NA_KH = 8
NA_KW = 16
GQA_KV_HEADS = 4
CONV_WIDTH = 3
ROPE_BASE = 10000.0
EPS = 1e-6
ALPHA = (2 * DEPTH) ** 0.25
LOG2E = math.log2(math.e)
Q_SCALE = HEAD_DIM ** -0.5 * LOG2E

LANES = 128
SUBLANES_F32 = 8
SUBLANES_BF16 = 16
MXU_DIM = 256
VMEM_LIMIT_BYTES = 56 * 1024 * 1024

PAIR = 2 * HEAD_DIM
assert PAIR == LANES

TM_LATENT = 512
TM_OUTPROJ = 1024
OUTPROJ_ROW_BLOCK = 256
TQ_GQA = 1024
TQ_DIFF = 2048
NA_ROWS_PER_BLOCK = 4
KEY_CHUNK = MXU_DIM
ONES_ROWS = SUBLANES_BF16
FF_CHUNK = MXU_DIM
ADA_TN = 1536
INPROJ_GROUP = MXU_DIM
HALO = SUBLANES_BF16
FFN_OUT_ROW_BLOCKS = 2


def _cparams():
    return pltpu.CompilerParams(vmem_limit_bytes=VMEM_LIMIT_BYTES)


def _const_spec(shape):
    nd = len(shape)
    return pl.BlockSpec(shape, lambda *_: (0,) * nd, pipeline_mode=pl.Buffered(1))


def _layer_norm(z, g, b):
    mu = jnp.mean(z, axis=-1, keepdims=True)
    d = z - mu
    var = jnp.mean(d * d, axis=-1, keepdims=True)
    return d * lax.rsqrt(var + EPS) * g + b


def _gelu_tanh(x):
    c = math.sqrt(2.0 / math.pi)
    return (0.5 * x) * (1.0 + jnp.tanh(x * (c + (0.044715 * c) * (x * x))))


def _group_sum_matrix():
    r = lax.broadcasted_iota(jnp.int32, (2 * LANES, LANES), 0)
    c = lax.broadcasted_iota(jnp.int32, (2 * LANES, LANES), 1)
    same = ((r % LANES) // HEAD_DIM) == (c // HEAD_DIM)
    return jnp.where(same, 1.0, 0.0).astype(BF16)


def _head_rms(xb, gm, g):
    x2 = xb * xb
    hi = x2.astype(BF16)
    lo = (x2 - hi.astype(F32)).astype(BF16)
    ssq = jnp.dot(jnp.concatenate([hi, lo], axis=1), gm, preferred_element_type=F32)
    return xb * lax.rsqrt(ssq * (1.0 / HEAD_DIM) + EPS) * g


def _rope(xb, cos, sin_a, sin_b):
    return (xb * cos
            + pltpu.roll(xb, LANES - HEAD_DIM // 4, 1) * sin_a
            + pltpu.roll(xb, HEAD_DIM // 4, 1) * sin_b)


def _split_heads_rows(q_pair):
    qf = q_pair.astype(F32)
    lane = lax.broadcasted_iota(jnp.int32, qf.shape, 1)
    lo = lane < HEAD_DIM
    return [jnp.where(lo, qf, 0.0).astype(BF16), jnp.where(lo, 0.0, qf).astype(BF16)]


def _attend(streams):
    n_chunks = len(streams[0][1])
    s_buf, e_buf = {}, {}
    m = [None] * len(streams)
    acc = [None] * len(streams)
    for t in range(n_chunks + 2):
        for i, (q_stack, chunks) in enumerate(streams):
            if t < n_chunks:
                k_c, _, bias_c = chunks[t]
                s = lax.dot_general(k_c, q_stack, (((1,), (1,)), ((), ())), preferred_element_type=F32)
                s_buf[i, t] = s if bias_c is None else s + bias_c
        for i, (q_stack, chunks) in enumerate(streams):
            if 0 <= t - 1 < n_chunks:
                s = s_buf.pop((i, t - 1))
                mc = jnp.max(s, axis=0, keepdims=True)
                m_new = mc if m[i] is None else jnp.maximum(m[i], mc)
                alpha = None if m[i] is None else jnp.exp2(m[i] - m_new)
                m[i] = m_new
                e_buf[i, t - 1] = (jnp.exp2(s - m_new).astype(BF16), alpha)
        for i, (q_stack, chunks) in enumerate(streams):
            if 0 <= t - 2 < n_chunks:
                e, alpha = e_buf.pop((i, t - 2))
                vt_c = chunks[t - 2][1]
                vt_aug = jnp.concatenate([vt_c, jnp.ones((ONES_ROWS, vt_c.shape[1]), BF16)], axis=0)
                pv = jnp.dot(vt_aug, e, preferred_element_type=F32)
                acc[i] = pv if alpha is None else alpha * acc[i] + pv
    outs = []
    for (q_stack, chunks), a in zip(streams, acc):
        dv = chunks[0][1].shape[0]
        outs.append(a[:dv] * (1.0 / a[dv:dv + 1]))
    return outs


def _key_chunks(kc_ref, vc_ref, kl_ref, vl_ref, k_cols=slice(None), v_rows=slice(None)):
    chunks = [(kc_ref[0, :, k_cols], vc_ref[0, v_rows, :], None)]
    if kl_ref is not None:
        n_lat = kl_ref.shape[1]
        for c0 in range(0, n_lat, KEY_CHUNK):
            chunks.append((kl_ref[0, c0:c0 + KEY_CHUNK, k_cols], vl_ref[0, v_rows, c0:c0 + KEY_CHUNK], None))
    return chunks


def _pick_head_rows(res, tq, p):
    row = lax.broadcasted_iota(jnp.int32, (LANES, tq), 0)
    a = res[:, (2 * p) * tq:(2 * p + 1) * tq]
    b = res[:, (2 * p + 1) * tq:(2 * p + 2) * tq]
    return jnp.where(row < HEAD_DIM, a, b).T


def _stack_head_rows(res, tq, p):
    return jnp.concatenate([res[:, (2 * p) * tq:(2 * p + 1) * tq],
                            res[:, (2 * p + 1) * tq:(2 * p + 2) * tq]], axis=0).T


def _ada_kernel(c_ref, w_ref, b_ref, o_ref):
    c = c_ref[...]
    cond = c * (1.0 / (1.0 + jnp.exp(-c)))
    o_ref[0] = jnp.dot(cond, w_ref[0], precision=lax.Precision.HIGHEST,
                       preferred_element_type=F32) + b_ref[0]


def _ada_call(cc, w_ada, b_ada):
    n_layers, d, n_out = w_ada.shape
    rows = cc.shape[0]
    return pl.pallas_call(
        _ada_kernel,
        grid=(n_layers, n_out // ADA_TN),
        in_specs=[
            pl.BlockSpec((rows, d), lambda l, n: (0, 0)),
            pl.BlockSpec((1, d, ADA_TN), lambda l, n: (l, 0, n)),
            pl.BlockSpec((1, 1, ADA_TN), lambda l, n: (l, 0, n)),
        ],
        out_specs=pl.BlockSpec((1, rows, ADA_TN), lambda l, n: (l, 0, n)),
        out_shape=jax.ShapeDtypeStruct((n_layers, rows, n_out), F32),
        compiler_params=_cparams(),
        name="adaln_mod",
    )(cc, w_ada, b_ada.reshape(n_layers, 1, n_out))


def _inproj_kernel(*refs, block_plan, rope):
    if rope:
        x_ref, mod_ref, wqk_ref, wvt_ref, g_ref, cos_ref, sa_ref, sb_ref, oqk_ref, ovt_ref = refs
    else:
        x_ref, mod_ref, wqk_ref, wvt_ref, g_ref, oqk_ref, ovt_ref = refs
    mod = mod_ref[0]
    h = (x_ref[0] * (1.0 + mod[1:2]) + mod[0:1]).astype(BF16)
    has_norm = any(p[0] is not None for p in block_plan)
    gm = _group_sum_matrix() if has_norm else None
    if rope:
        cos, sa, sb = cos_ref[...], sa_ref[...], sb_ref[...]

    def v_proj():
        vt = lax.dot_general(wvt_ref[...], h, (((1,), (1,)), ((), ())), preferred_element_type=F32)
        ovt_ref[0] = vt.astype(BF16)

    def epilogue(j0, qk):
        for jj in range(qk.shape[1] // LANES):
            gain_row, do_rope, scale = block_plan[j0 + jj]
            blk = qk[:, jj * LANES:(jj + 1) * LANES]
            if gain_row is not None:
                blk = _head_rms(blk, gm, g_ref[gain_row:gain_row + 1, :])
            if do_rope and rope:
                blk = _rope(blk, cos, sa, sb)
            if scale != 1.0:
                blk = blk * scale
            oqk_ref[0, :, (j0 + jj) * LANES:(j0 + jj + 1) * LANES] = blk.astype(BF16)

    if not has_norm:
        qk = jnp.dot(h, wqk_ref[...], preferred_element_type=F32)
        v_proj()
        epilogue(0, qk)
        return
    bpg = INPROJ_GROUP // LANES
    n_groups = len(block_plan) // bpg
    prev = None
    for g in range(n_groups + 1):
        cur = None
        if g < n_groups:
            cur = jnp.dot(h, wqk_ref[:, g * INPROJ_GROUP:(g + 1) * INPROJ_GROUP], preferred_element_type=F32)
        else:
            v_proj()
        if prev is not None:
            epilogue((g - 1) * bpg, prev)
        prev = cur


def _inproj_call(x, mod, mod_row, wqk, wvt, gains, block_plan, rope_tabs, tm, name):
    b, t, d = x.shape
    wq = wqk.shape[1]
    wv = wvt.shape[0]
    rope = rope_tabs is not None
    mod_map = (lambda bi, i: (bi, 0, 0)) if mod_row is None else (lambda bi, i: (mod_row, 0, 0))
    in_specs = [
        pl.BlockSpec((1, tm, d), lambda bi, i: (bi, i, 0)),
        pl.BlockSpec((1,) + mod.shape[1:], mod_map),
        _const_spec(wqk.shape),
        _const_spec(wvt.shape),
        _const_spec(gains.shape),
    ]
    args = [x, mod, wqk, wvt, gains]
    if rope:
        in_specs += [pl.BlockSpec((tm, LANES), lambda bi, i: (i, 0))] * 3
        args += list(rope_tabs)
    return pl.pallas_call(
        functools.partial(_inproj_kernel, block_plan=block_plan, rope=rope),
        grid=(b, t // tm),
        in_specs=in_specs,
        out_specs=[
            pl.BlockSpec((1, tm, wq), lambda bi, i: (bi, i, 0)),
            pl.BlockSpec((1, wv, tm), lambda bi, i: (bi, 0, i)),
        ],
        out_shape=[jax.ShapeDtypeStruct((b, t, wq), BF16), jax.ShapeDtypeStruct((b, wv, t), BF16)],
        compiler_params=_cparams(),
        name=name,
    )(*args)


def _gqa_kernel(*refs, tq, latent_keys):
    if latent_keys:
        q_ref, kc_ref, vc_ref, kl_ref, vl_ref, o_ref = refs
    else:
        q_ref, kc_ref, vc_ref, o_ref = refs
        kl_ref = vl_ref = None
    n_groups = q_ref.shape[2] // (2 * LANES)
    streams = []
    for g in range(n_groups):
        q = q_ref[0, :, g * 2 * LANES:(g + 1) * 2 * LANES]
        parts = _split_heads_rows(q[:, :LANES]) + _split_heads_rows(q[:, LANES:])
        chunks = _key_chunks(kc_ref, vc_ref, kl_ref, vl_ref, slice(g * LANES, (g + 1) * LANES),
                             slice(g * HEAD_DIM, (g + 1) * HEAD_DIM))
        streams.append((jnp.concatenate(parts, axis=0), chunks))
    for g, res in enumerate(_attend(streams)):
        o_ref[0, :, g * 2 * LANES:(g + 1) * 2 * LANES] = jnp.concatenate(
            [_stack_head_rows(res, tq, 0), _stack_head_rows(res, tq, 1)], axis=1).astype(BF16)


def _gqa_call(qk_q, qk_ctx, vt_ctx, qk_lat, vt_lat, tq, groups, name):
    b, t, _ = qk_q.shape
    n_ctx = qk_ctx.shape[1]
    k_blk0 = (GQA_KV_HEADS * 4 * HEAD_DIM) // (groups * LANES)
    latent_keys = qk_lat is not None
    in_specs = [
        pl.BlockSpec((1, tq, groups * 2 * LANES), lambda bi, g, i: (bi, i, g)),
        pl.BlockSpec((1, n_ctx, groups * LANES), lambda bi, g, i: (bi, 0, k_blk0 + g)),
        pl.BlockSpec((1, groups * HEAD_DIM, n_ctx), lambda bi, g, i: (bi, g, 0)),
    ]
    args = [qk_q, qk_ctx, vt_ctx]
    if latent_keys:
        n_lat = qk_lat.shape[1]
        in_specs += [
            pl.BlockSpec((1, n_lat, groups * LANES), lambda bi, g, i: (bi, 0, k_blk0 + g)),
            pl.BlockSpec((1, groups * HEAD_DIM, n_lat), lambda bi, g, i: (bi, g, 0)),
        ]
        args += [qk_lat, vt_lat]
    return pl.pallas_call(
        functools.partial(_gqa_kernel, tq=tq, latent_keys=latent_keys),
        grid=(b, GQA_KV_HEADS // groups, t // tq),
        in_specs=in_specs,
        out_specs=pl.BlockSpec((1, tq, groups * 2 * LANES), lambda bi, g, i: (bi, i, g)),
        out_shape=jax.ShapeDtypeStruct((b, t, GQA_KV_HEADS * 2 * LANES), BF16),
        compiler_params=_cparams(),
        name=name,
    )(*args)


def _diff_kernel(*refs, tq, latent_keys, lambda_init):
    if latent_keys:
        q_ref, kc_ref, vc_ref, kl_ref, vl_ref, lam_ref, g_ref, o_ref = refs
    else:
        q_ref, kc_ref, vc_ref, lam_ref, g_ref, o_ref = refs
        kl_ref = vl_ref = None
    n_heads = q_ref.shape[2] // LANES
    streams = []
    for hd in range(n_heads):
        cols = slice(hd * LANES, (hd + 1) * LANES)
        q_stack = jnp.concatenate(_split_heads_rows(q_ref[0, :, cols]), axis=0)
        streams.append((q_stack, _key_chunks(kc_ref, vc_ref, kl_ref, vl_ref, cols, cols)))
    lv = lam_ref[...]
    lam = (jnp.exp(jnp.sum(lv[0:1] * lv[1:2], axis=1, keepdims=True))
           - jnp.exp(jnp.sum(lv[2:3] * lv[3:4], axis=1, keepdims=True)) + lambda_init)
    for hd, res in enumerate(_attend(streams)):
        o = (res[:, :tq] - lam * res[:, tq:]).T
        o = o * lax.rsqrt(jnp.mean(o * o, axis=-1, keepdims=True) + EPS) * g_ref[...]
        o_ref[0, :, hd * LANES:(hd + 1) * LANES] = (o * (1.0 - lambda_init)).astype(BF16)


def _diff_call(qk_q, qk_ctx, vt_ctx, qk_lat, vt_lat, lam_vec, subln_g, lambda_init, tq, heads, name):
    b, t, wqk = qk_q.shape
    n_ctx = qk_ctx.shape[1]
    n_heads = wqk // (4 * LANES)
    n_steps = n_heads // heads
    q_blk0, k_blk0, v_blk0 = 2 * n_steps, 3 * n_steps, n_steps
    w = heads * LANES
    latent_keys = qk_lat is not None
    in_specs = [
        pl.BlockSpec((1, tq, w), lambda bi, h, i: (bi, i, q_blk0 + h)),
        pl.BlockSpec((1, n_ctx, w), lambda bi, h, i: (bi, 0, k_blk0 + h)),
        pl.BlockSpec((1, w, n_ctx), lambda bi, h, i: (bi, v_blk0 + h, 0)),
    ]
    args = [qk_q, qk_ctx, vt_ctx]
    if latent_keys:
        n_lat = qk_lat.shape[1]
        in_specs += [
            pl.BlockSpec((1, n_lat, w), lambda bi, h, i: (bi, 0, k_blk0 + h)),
            pl.BlockSpec((1, w, n_lat), lambda bi, h, i: (bi, v_blk0 + h, 0)),
        ]
        args += [qk_lat, vt_lat]
    in_specs += [_const_spec(lam_vec.shape), _const_spec((1, LANES))]
    args += [lam_vec, subln_g.reshape(1, LANES)]
    return pl.pallas_call(
        functools.partial(_diff_kernel, tq=tq, latent_keys=latent_keys, lambda_init=lambda_init),
        grid=(b, n_steps, t // tq),
        in_specs=in_specs,
        out_specs=pl.BlockSpec((1, tq, w), lambda bi, h, i: (bi, i, h)),
        out_shape=jax.ShapeDtypeStruct((b, t, n_heads * LANES), BF16),
        compiler_params=_cparams(),
        name=name,
    )(*args)


def _na_kernel(*refs, n_local):
    q_ref, kc_ref, vc_ref = refs[:3]
    k_refs = refs[3:3 + n_local]
    v_refs = refs[3 + n_local:3 + 2 * n_local]
    bias_ref = refs[3 + 2 * n_local] if n_local else None
    o_ref = refs[-1]
    tq = q_ref.shape[1]
    n_pairs = q_ref.shape[2] // LANES
    streams = []
    for p in range(n_pairs):
        cols = slice(p * LANES, (p + 1) * LANES)
        q_stack = jnp.concatenate(_split_heads_rows(q_ref[0, :, cols]), axis=0)
        chunks = [(kc_ref[0, :, cols], vc_ref[0, cols, :], None)]
        for c in range(n_local):
            chunks.append((k_refs[c][0, :, cols], v_refs[c][0, cols, :], bias_ref[0, p, c * tq:(c + 1) * tq, :]))
        streams.append((q_stack, chunks))
    for p, res in enumerate(_attend(streams)):
        o_ref[0, :, p * LANES:(p + 1) * LANES] = _pick_head_rows(res, tq, 0).astype(BF16)


def _na_call(qk_lat, vt_lat, qk_ctx, vt_ctx, bias_tab, name):
    b, s, _ = qk_lat.shape
    n_ctx = qk_ctx.shape[1]
    tq = NA_ROWS_PER_BLOCK * GRID_W
    nj = s // tq
    n_pairs = bias_tab.shape[1]
    w = n_pairs * LANES
    n_local = 3

    def tb(j):
        return jnp.clip(j - 1, 0, nj - n_local)

    def cls(j):
        return jnp.where(j == 0, 0, jnp.where(j == nj - 1, 2, 1))

    k_specs = [pl.BlockSpec((1, tq, w), functools.partial(lambda j, bi, d: (bi, tb(j) + d, 1), d=d))
               for d in range(n_local)]
    v_specs = [pl.BlockSpec((1, w, tq), functools.partial(lambda j, bi, d: (bi, 0, tb(j) + d), d=d))
               for d in range(n_local)]
    return pl.pallas_call(
        functools.partial(_na_kernel, n_local=n_local),
        grid=(nj, b),
        in_specs=[
            pl.BlockSpec((1, tq, w), lambda j, bi: (bi, j, 0)),
            pl.BlockSpec((1, n_ctx, w), lambda j, bi: (bi, 0, 1)),
            pl.BlockSpec((1, w, n_ctx), lambda j, bi: (bi, 0, 0)),
            *k_specs, *v_specs,
            pl.BlockSpec((1, n_pairs, n_local * tq, 2 * tq), lambda j, bi: (cls(j), 0, 0, 0)),
        ],
        out_specs=pl.BlockSpec((1, tq, w), lambda j, bi: (bi, j, 0)),
        out_shape=jax.ShapeDtypeStruct((b, s, w), BF16),
        compiler_params=_cparams(),
        name=name,
    )(qk_lat, qk_ctx, vt_ctx, *([qk_lat] * n_local), *([vt_lat] * n_local), bias_tab)


def _na_ctx_call(qk_ctx, vt_ctx, n_pairs, name):
    b, n_ctx, _ = qk_ctx.shape
    w = n_pairs * LANES
    return pl.pallas_call(
        functools.partial(_na_kernel, n_local=0),
        grid=(b,),
        in_specs=[
            pl.BlockSpec((1, n_ctx, w), lambda bi: (bi, 0, 0)),
            pl.BlockSpec((1, n_ctx, w), lambda bi: (bi, 0, 1)),
            pl.BlockSpec((1, w, n_ctx), lambda bi: (bi, 0, 0)),
        ],
        out_specs=pl.BlockSpec((1, n_ctx, w), lambda bi: (bi, 0, 0)),
        out_shape=jax.ShapeDtypeStruct((b, n_ctx, w), BF16),
        compiler_params=_cparams(),
        name=name,
    )(qk_ctx, qk_ctx, vt_ctx)


def _na_bias_tables(rpb, seq):
    n_heads = rpb.shape[0]
    rows = seq // GRID_W
    tq = NA_ROWS_PER_BLOCK * GRID_W
    nj = seq // tq
    kh = min(NA_KH, rows)
    n_dc = rpb.shape[2]
    n_pairs = n_heads // 2
    pad = GRID_W - 1 - (n_dc - 1) // 2
    rpb_pad = jnp.pad(rpb.astype(F32), ((0, 0), (0, 0), (pad, pad)), mode="edge")
    col_tab = jnp.stack([rpb_pad[:, :, GRID_W - 1 - qc:2 * GRID_W - 1 - qc] for qc in range(GRID_W)], axis=-1)
    cols = np.arange(GRID_W)
    cs = np.clip(cols - NA_KW // 2, 0, GRID_W - NA_KW)
    col_ok = (cols[:, None] >= cs[None, :]) & (cols[:, None] < cs[None, :] + NA_KW)
    col_tab = jnp.where(col_ok[None, None], col_tab, -jnp.inf)
    col_tab = col_tab.reshape(n_pairs, 2, col_tab.shape[1], GRID_W, GRID_W)
    masked = jnp.full((n_pairs, GRID_W, GRID_W), -jnp.inf, F32)
    n_kr = 3 * NA_ROWS_PER_BLOCK
    tabs = []
    for j in (0, 1, nj - 1):
        ws = NA_ROWS_PER_BLOCK * int(np.clip(j - 1, 0, nj - 3))
        key_rows = []
        for a in range(n_kr):
            blocks = []
            for hp in range(2):
                for qr in range(NA_ROWS_PER_BLOCK):
                    kr, r = ws + a, NA_ROWS_PER_BLOCK * j + qr
                    rs = int(np.clip(r - kh // 2, 0, rows - kh))
                    row_ok = rs <= kr < rs + kh
                    blocks.append(col_tab[:, hp, kr - r + NA_KH - 1] if row_ok else masked)
            key_rows.append(jnp.concatenate(blocks, axis=-1))
        tabs.append(jnp.concatenate(key_rows, axis=1))
    return jnp.stack(tabs)


def _outproj_kernel(*refs, n_att):
    att_refs = refs[:n_att]
    x_ref, mod_ref, w_ref, lng_ref, lnb_ref, o_ref = refs[n_att:]
    gate = mod_ref[0][2:3]
    tm = x_ref.shape[1]
    rb = min(OUTPROJ_ROW_BLOCK, tm)

    def proj(rs):
        y = None
        k0 = 0
        for a_ref in att_refs:
            kw = a_ref.shape[2]
            part = jnp.dot(a_ref[0, rs, :], w_ref[k0:k0 + kw, :], preferred_element_type=F32)
            y = part if y is None else y + part
            k0 += kw
        return y

    prev = None
    for r in range(tm // rb + 1):
        cur = proj(slice(r * rb, (r + 1) * rb)) if r < tm // rb else None
        if prev is not None:
            rs = slice((r - 1) * rb, r * rb)
            z = ALPHA * x_ref[0, rs, :] + gate * prev
            o_ref[0, rs, :] = _layer_norm(z, lng_ref[...], lnb_ref[...])
        prev = cur


def _outproj_call(atts, x, mod, mod_row, w_o, ln_g, ln_b, tm, name):
    b, t, d = x.shape
    mod_map = (lambda bi, i: (bi, 0, 0)) if mod_row is None else (lambda bi, i: (mod_row, 0, 0))
    in_specs = [pl.BlockSpec((1, tm, a.shape[2]), lambda bi, i: (bi, i, 0)) for a in atts]
    in_specs += [
        pl.BlockSpec((1, tm, d), lambda bi, i: (bi, i, 0)),
        pl.BlockSpec((1,) + mod.shape[1:], mod_map),
        _const_spec(w_o.shape),
        _const_spec((1, d)),
        _const_spec((1, d)),
    ]
    return pl.pallas_call(
        functools.partial(_outproj_kernel, n_att=len(atts)),
        grid=(b, t // tm),
        in_specs=in_specs,
        out_specs=pl.BlockSpec((1, tm, d), lambda bi, i: (bi, i, 0)),
        out_shape=jax.ShapeDtypeStruct((b, t, d), F32),
        compiler_params=_cparams(),
        name=name,
    )(*atts, x, mod, w_o, ln_g.reshape(1, d), ln_b.reshape(1, d))


def _ffn_kernel(*refs, tm, halo):
    if halo:
        (x_ref, xp_ref, xn_ref, mod_ref, wup_ref, cw_ref, cb_ref, wdn_ref, lng_ref, lnb_ref,
         o_ref, hs_ref, act_ref) = refs
    else:
        x_ref, mod_ref, wup_ref, cw_ref, cb_ref, wdn_ref, lng_ref, lnb_ref, o_ref, hs_ref, act_ref = refs
    d = x_ref.shape[2]
    d_ff = wdn_ref.shape[0]
    rows = tm + HALO
    mod = mod_ref[0]
    shift, scale, gate = mod[3:4], mod[4:5], mod[5:6]
    x = x_ref[0]
    hs_ref[0:tm, :] = (x * (1.0 + scale) + shift).astype(BF16)
    zeros8 = jnp.zeros((SUBLANES_F32, d), F32)
    if halo:
        i = pl.program_id(1)
        last = pl.num_programs(1) - 1
        hp = jnp.where(i > 0, xp_ref[0] * (1.0 + scale) + shift, 0.0)
        hn = jnp.where(i < last, xn_ref[0] * (1.0 + scale) + shift, 0.0)
    else:
        hp = hn = zeros8
    hs_ref[tm:rows, :] = jnp.concatenate([hn, hp], axis=0).astype(BF16)
    hs = hs_ref[...]
    n_chunks = d_ff // FF_CHUNK

    def up_proj(cols):
        return jnp.dot(hs, wup_ref[:, cols], preferred_element_type=F32)

    def conv(up, cols):
        w = cw_ref[:, cols]
        return (pltpu.roll(up, 1, 0)[0:tm] * w[0:1] + up[0:tm] * w[1:2]
                + pltpu.roll(up, rows - 1, 0)[0:tm] * w[2:3] + cb_ref[:, cols])

    ups = {}
    for t in range(n_chunks + 1):
        if t < n_chunks:
            a_cols = slice(t * FF_CHUNK, (t + 1) * FF_CHUNK)
            g_cols = slice(d_ff + t * FF_CHUNK, d_ff + (t + 1) * FF_CHUNK)
            ups[t] = (up_proj(a_cols), up_proj(g_cols), a_cols, g_cols)
        if t >= 1:
            up_a, up_g, a_cols, g_cols = ups.pop(t - 1)
            act_ref[:, a_cols] = (_gelu_tanh(conv(up_g, g_cols)) * conv(up_a, a_cols)).astype(BF16)
    rb = tm // FFN_OUT_ROW_BLOCKS
    prev = None
    for r in range(FFN_OUT_ROW_BLOCKS + 1):
        cur = None
        if r < FFN_OUT_ROW_BLOCKS:
            cur = jnp.dot(act_ref[r * rb:(r + 1) * rb, :], wdn_ref[...], preferred_element_type=F32)
        if prev is not None:
            rs = slice((r - 1) * rb, r * rb)
            z = ALPHA * x_ref[0, rs, :] + gate * prev
            o_ref[0, rs, :] = _layer_norm(z, lng_ref[...], lnb_ref[...])
        prev = cur


def _ffn_call(x, mod, mod_row, w_up, conv_w, conv_b, w_down, ln_g, ln_b, tm, name):
    b, t, d = x.shape
    halo = t > tm
    mod_map = (lambda bi, i: (bi, 0, 0)) if mod_row is None else (lambda bi, i: (mod_row, 0, 0))
    in_specs = [pl.BlockSpec((1, tm, d), lambda bi, i: (bi, i, 0))]
    args = [x]
    if halo:
        r8 = tm // SUBLANES_F32
        n8 = t // SUBLANES_F32
        in_specs += [
            pl.BlockSpec((1, SUBLANES_F32, d), lambda bi, i: (bi, jnp.maximum(i * r8 - 1, 0), 0)),
            pl.BlockSpec((1, SUBLANES_F32, d), lambda bi, i: (bi, jnp.minimum((i + 1) * r8, n8 - 1), 0)),
        ]
        args += [x, x]
    in_specs += [
        pl.BlockSpec((1,) + mod.shape[1:], mod_map),
        _const_spec(w_up.shape),
        _const_spec(conv_w.shape),
        _const_spec(conv_b.shape),
        _const_spec(w_down.shape),
        _const_spec((1, d)),
        _const_spec((1, d)),
    ]
    args += [mod, w_up, conv_w, conv_b, w_down, ln_g.reshape(1, d), ln_b.reshape(1, d)]
    return pl.pallas_call(
        functools.partial(_ffn_kernel, tm=tm, halo=halo),
        grid=(b, t // tm),
        in_specs=in_specs,
        out_specs=pl.BlockSpec((1, tm, d), lambda bi, i: (bi, i, 0)),
        out_shape=jax.ShapeDtypeStruct((b, t, d), F32),
        scratch_shapes=[pltpu.VMEM((tm + HALO, d), BF16), pltpu.VMEM((tm, w_down.shape[0]), BF16)],
        compiler_params=_cparams(),
        name=name,
    )(*args)


def _rope_tables(seq):
    n_freq = HEAD_DIM // 4
    inv = ROPE_BASE ** (-jnp.arange(n_freq, dtype=F32) / n_freq)
    t = jnp.arange(seq)
    row = (t // GRID_W).astype(F32)
    col = (t % GRID_W).astype(F32)
    ar = row[:, None] * inv
    ac = col[:, None] * inv
    ang = jnp.concatenate([ar, ar, ac, ac], -1)
    cos, sin = jnp.cos(ang), jnp.sin(ang)
    even_chunk = ((np.arange(HEAD_DIM) // n_freq) % 2 == 0)[None, :]
    sin_a = jnp.where(even_chunk, -sin, 0.0)
    sin_b = jnp.where(even_chunk, 0.0, sin)
    return tuple(jnp.concatenate([a, a], axis=1) for a in (cos, sin_a, sin_b))


def _dup_heads(w, n_heads):
    d = w.shape[0]
    w = w.reshape(d, n_heads, HEAD_DIM)
    return jnp.concatenate([w, w], axis=-1).reshape(d, n_heads * PAIR)


def kernel(x, c, ctx, c_ctx, w_ada, b_ada, ln_g, ln_b, w_in_ab, w_o_ab, na_rpb, diff_lambda, diff_subln,
           w_in_c, w_o_c, gqa_qk_norm, w_up, conv_w, conv_b, w_down):
    b, s, d = x.shape
    n_ctx = ctx.shape[1]
    d_ff = w_down.shape[1]
    na_w = na_rpb.shape[1] * HEAD_DIM
    diff_w = d - na_w
    gqa_kw = GQA_KV_HEADS * HEAD_DIM
    tm_lat = min(TM_LATENT, s)

    mod_rows = -(-(b + 1) // SUBLANES_F32) * SUBLANES_F32
    ctx_row = b
    cc = jnp.concatenate([c, c_ctx[None, :], jnp.zeros((mod_rows - b - 1, d), F32)], axis=0)
    mod_all = _ada_call(cc, w_ada, b_ada)
    rope_tabs = _rope_tables(s)
    ones_gain = jnp.ones((2, LANES), F32)

    xc = ctx
    for l in range(DEPTH):
        need_ctx = l < DEPTH - 1
        i = l // 2
        mod = mod_all[l].reshape(mod_rows, 6, d)
        if l % 2 == 0:
            lambda_init = 0.8 - 0.6 * math.exp(-0.3 * l)
            w = w_in_ab[i]
            o0 = 3 * na_w
            wqk = jnp.concatenate([w[:, :2 * na_w], w[:, o0:o0 + 2 * diff_w]], axis=1).astype(BF16)
            wvt = jnp.concatenate([w[:, 2 * na_w:o0], w[:, o0 + 2 * diff_w:]], axis=1).T.astype(BF16)
            nb = na_w // LANES
            plan = ([(None, False, Q_SCALE)] * nb + [(None, False, 1.0)] * nb
                    + [(None, True, Q_SCALE)] * (diff_w // LANES) + [(None, True, 1.0)] * (diff_w // LANES))
            gains = ones_gain
        else:
            w = w_in_c[i]
            q_w = d
            wqk = jnp.concatenate([w[:, :q_w], _dup_heads(w[:, q_w:q_w + gqa_kw], GQA_KV_HEADS)],
                                  axis=1).astype(BF16)
            wvt = w[:, q_w + gqa_kw:].T.astype(BF16)
            plan = [(0, True, Q_SCALE)] * (q_w // LANES) + [(1, True, 1.0)] * GQA_KV_HEADS
            gains = jnp.concatenate([gqa_qk_norm[i], gqa_qk_norm[i]], axis=1)
        plan = tuple(plan)

        qk_lat, vt_lat = _inproj_call(x, mod, None, wqk, wvt, gains, plan, rope_tabs, tm_lat, f"inproj_lat_{l}")
        qk_ctx, vt_ctx = _inproj_call(xc, mod, ctx_row, wqk, wvt, gains, plan, None, n_ctx, f"inproj_ctx_{l}")

        if l % 2 == 0:
            bias_tab = _na_bias_tables(na_rpb[i] * LOG2E, s)
            atts = [
                _na_call(qk_lat, vt_lat, qk_ctx, vt_ctx, bias_tab, f"na_lat_{l}"),
                _diff_call(qk_lat, qk_ctx, vt_ctx, qk_lat, vt_lat, diff_lambda[i], diff_subln[i], lambda_init,
                           min(TQ_DIFF, s), 1, f"diff_lat_{l}"),
            ]
            w_o = w_o_ab[i].astype(BF16)
            if need_ctx:
                atts_c = [
                    _na_ctx_call(qk_ctx, vt_ctx, na_w // LANES, f"na_ctx_{l}"),
                    _diff_call(qk_ctx, qk_ctx, vt_ctx, None, None, diff_lambda[i], diff_subln[i], lambda_init,
                               n_ctx, diff_w // LANES, f"diff_ctx_{l}"),
                ]
        else:
            atts = [_gqa_call(qk_lat, qk_ctx, vt_ctx, qk_lat, vt_lat, min(TQ_GQA, s), 1, f"gqa_lat_{l}")]
            w_o = w_o_c[i].astype(BF16)
            if need_ctx:
                atts_c = [_gqa_call(qk_ctx, qk_ctx, vt_ctx, None, None, n_ctx, GQA_KV_HEADS, f"gqa_ctx_{l}")]

        w_up_l = w_up[l].astype(BF16)
        conv_w_l = conv_w[l]
        conv_b_l = conv_b[l].reshape(1, 2 * d_ff)
        w_down_l = w_down[l].astype(BF16)

        x = _outproj_call(atts, x, mod, None, w_o, ln_g[l, 0], ln_b[l, 0], min(TM_OUTPROJ, s), f"outproj_lat_{l}")
        x = _ffn_call(x, mod, None, w_up_l, conv_w_l, conv_b_l, w_down_l, ln_g[l, 1], ln_b[l, 1],
                      tm_lat, f"ffn_lat_{l}")
        if need_ctx:
            xc = _outproj_call(atts_c, xc, mod, ctx_row, w_o, ln_g[l, 0], ln_b[l, 0], n_ctx, f"outproj_ctx_{l}")
            xc = _ffn_call(xc, mod, ctx_row, w_up_l, conv_w_l, conv_b_l, w_down_l, ln_g[l, 1], ln_b[l, 1],
                           n_ctx, f"ffn_ctx_{l}")
    return x
```

```python
import functools
import math

import numpy as np
import jax
import jax.numpy as jnp
from jax import lax
from jax.experimental import pallas as pl
from jax.experimental.pallas import tpu as pltpu

F32 = jnp.float32
BF16 = jnp.bfloat16

DEPTH = 4
GRID_W = 64
HEAD_DIM = 64
NA_KH = 8
NA_KW = 16
GQA_KV_HEADS = 4
CONV_WIDTH = 3
ROPE_BASE = 10000.0
EPS = 1e-6
ALPHA = (2 * DEPTH) ** 0.25
LOG2E = math.log2(math.e)
Q_SCALE = HEAD_DIM ** -0.5 * LOG2E

LANES = 128
SUBLANES_F32 = 8
SUBLANES_BF16 = 16
MXU_DIM = 256
VMEM_LIMIT_BYTES = 56 * 1024 * 1024

PAIR = 2 * HEAD_DIM
assert PAIR == LANES

TM_LATENT = 512
TM_OUTPROJ = 1024
OUTPROJ_ROW_BLOCK = 256
TQ_GQA = 1024
TQ_DIFF = 2048
NA_ROWS_PER_BLOCK = 4
NA_BATCH_PER_STEP = 2
KEY_CHUNK = MXU_DIM
ONES_ROWS = SUBLANES_BF16
FF_CHUNK = MXU_DIM
ADA_TN = 1536
INPROJ_GROUP = MXU_DIM
HALO = SUBLANES_BF16
FFN_OUT_ROW_BLOCKS = 2


def _cparams():
    return pltpu.CompilerParams(vmem_limit_bytes=VMEM_LIMIT_BYTES)


def _const_spec(shape):
    nd = len(shape)
    return pl.BlockSpec(shape, lambda *_: (0,) * nd, pipeline_mode=pl.Buffered(1))


def _layer_norm(z, g, b):
    mu = jnp.mean(z, axis=-1, keepdims=True)
    d = z - mu
    var = jnp.mean(d * d, axis=-1, keepdims=True)
    return d * lax.rsqrt(var + EPS) * g + b


def _gelu_tanh(x):
    c = math.sqrt(2.0 / math.pi)
    return (0.5 * x) * (1.0 + jnp.tanh(x * (c + (0.044715 * c) * (x * x))))


def _group_sum_matrix():
    r = lax.broadcasted_iota(jnp.int32, (2 * LANES, LANES), 0)
    c = lax.broadcasted_iota(jnp.int32, (2 * LANES, LANES), 1)
    same = ((r % LANES) // HEAD_DIM) == (c // HEAD_DIM)
    return jnp.where(same, 1.0, 0.0).astype(BF16)


def _head_rms(xb, gm, g):
    x2 = xb * xb
    hi = x2.astype(BF16)
    lo = (x2 - hi.astype(F32)).astype(BF16)
    ssq = jnp.dot(jnp.concatenate([hi, lo], axis=1), gm, preferred_element_type=F32)
    return xb * lax.rsqrt(ssq * (1.0 / HEAD_DIM) + EPS) * g


def _rope(xb, cos, sin_a, sin_b):
    return (xb * cos
            + pltpu.roll(xb, LANES - HEAD_DIM // 4, 1) * sin_a
            + pltpu.roll(xb, HEAD_DIM // 4, 1) * sin_b)


def _split_heads_rows(q_pair):
    qf = q_pair.astype(F32)
    lane = lax.broadcasted_iota(jnp.int32, qf.shape, 1)
    lo = lane < HEAD_DIM
    return [jnp.where(lo, qf, 0.0).astype(BF16), jnp.where(lo, 0.0, qf).astype(BF16)]


def _attend(streams):
    n_chunks = len(streams[0][1])
    s_buf, e_buf = {}, {}
    m = [None] * len(streams)
    acc = [None] * len(streams)
    for t in range(n_chunks + 2):
        for i, (q_stack, chunks) in enumerate(streams):
            if t < n_chunks:
                k_c, _, bias_c = chunks[t]
                s = lax.dot_general(k_c, q_stack, (((1,), (1,)), ((), ())), preferred_element_type=F32)
                s_buf[i, t] = s if bias_c is None else s + bias_c
        for i, (q_stack, chunks) in enumerate(streams):
            if 0 <= t - 1 < n_chunks:
                s = s_buf.pop((i, t - 1))
                mc = jnp.max(s, axis=0, keepdims=True)
                m_new = mc if m[i] is None else jnp.maximum(m[i], mc)
                alpha = None if m[i] is None else jnp.exp2(m[i] - m_new)
                m[i] = m_new
                e_buf[i, t - 1] = (jnp.exp2(s - m_new).astype(BF16), alpha)
        for i, (q_stack, chunks) in enumerate(streams):
            if 0 <= t - 2 < n_chunks:
                e, alpha = e_buf.pop((i, t - 2))
                vt_c = chunks[t - 2][1]
                vt_aug = jnp.concatenate([vt_c, jnp.ones((ONES_ROWS, vt_c.shape[1]), BF16)], axis=0)
                pv = jnp.dot(vt_aug, e, preferred_element_type=F32)
                acc[i] = pv if alpha is None else alpha * acc[i] + pv
    outs = []
    for (q_stack, chunks), a in zip(streams, acc):
        dv = chunks[0][1].shape[0]
        outs.append(a[:dv] * (1.0 / a[dv:dv + 1]))
    return outs


def _key_chunks(kc_ref, vc_ref, kl_ref, vl_ref, k_cols=slice(None), v_rows=slice(None)):
    chunks = [(kc_ref[0, :, k_cols], vc_ref[0, v_rows, :], None)]
    if kl_ref is not None:
        n_lat = kl_ref.shape[1]
        for c0 in range(0, n_lat, KEY_CHUNK):
            chunks.append((kl_ref[0, c0:c0 + KEY_CHUNK, k_cols], vl_ref[0, v_rows, c0:c0 + KEY_CHUNK], None))
    return chunks


def _pick_head_rows(res, tq, p):
    row = lax.broadcasted_iota(jnp.int32, (LANES, tq), 0)
    a = res[:, (2 * p) * tq:(2 * p + 1) * tq]
    b = res[:, (2 * p + 1) * tq:(2 * p + 2) * tq]
    return jnp.where(row < HEAD_DIM, a, b).T


def _stack_head_rows(res, tq, p):
    return jnp.concatenate([res[:, (2 * p) * tq:(2 * p + 1) * tq],
                            res[:, (2 * p + 1) * tq:(2 * p + 2) * tq]], axis=0).T


def _ada_kernel(c_ref, w_ref, b_ref, o_ref):
    c = c_ref[...]
    cond = c * (1.0 / (1.0 + jnp.exp(-c)))
    o_ref[0] = jnp.dot(cond, w_ref[0], precision=lax.Precision.HIGHEST,
                       preferred_element_type=F32) + b_ref[0]


def _ada_call(cc, w_ada, b_ada):
    n_layers, d, n_out = w_ada.shape
    rows = cc.shape[0]
    return pl.pallas_call(
        _ada_kernel,
        grid=(n_layers, n_out // ADA_TN),
        in_specs=[
            pl.BlockSpec((rows, d), lambda l, n: (0, 0)),
            pl.BlockSpec((1, d, ADA_TN), lambda l, n: (l, 0, n)),
            pl.BlockSpec((1, 1, ADA_TN), lambda l, n: (l, 0, n)),
        ],
        out_specs=pl.BlockSpec((1, rows, ADA_TN), lambda l, n: (l, 0, n)),
        out_shape=jax.ShapeDtypeStruct((n_layers, rows, n_out), F32),
        compiler_params=_cparams(),
        name="adaln_mod",
    )(cc, w_ada, b_ada.reshape(n_layers, 1, n_out))


def _inproj_kernel(*refs, block_plan, rope):
    if rope:
        x_ref, mod_ref, wqk_ref, wvt_ref, g_ref, cos_ref, sa_ref, sb_ref, oqk_ref, ovt_ref = refs
    else:
        x_ref, mod_ref, wqk_ref, wvt_ref, g_ref, oqk_ref, ovt_ref = refs
    mod = mod_ref[0]
    h = (x_ref[0] * (1.0 + mod[1:2]) + mod[0:1]).astype(BF16)
    has_norm = any(p[0] is not None for p in block_plan)
    gm = _group_sum_matrix() if has_norm else None
    if rope:
        cos, sa, sb = cos_ref[...], sa_ref[...], sb_ref[...]

    def v_proj():
        vt = lax.dot_general(wvt_ref[...], h, (((1,), (1,)), ((), ())), preferred_element_type=F32)
        ovt_ref[0] = vt.astype(BF16)

    def epilogue(j0, qk):
        for jj in range(qk.shape[1] // LANES):
            gain_row, do_rope, scale = block_plan[j0 + jj]
            blk = qk[:, jj * LANES:(jj + 1) * LANES]
            if gain_row is not None:
                blk = _head_rms(blk, gm, g_ref[gain_row:gain_row + 1, :])
            if do_rope and rope:
                blk = _rope(blk, cos, sa, sb)
            if scale != 1.0:
                blk = blk * scale
            oqk_ref[0, :, (j0 + jj) * LANES:(j0 + jj + 1) * LANES] = blk.astype(BF16)

    if not has_norm:
        qk = jnp.dot(h, wqk_ref[...], preferred_element_type=F32)
        v_proj()
        epilogue(0, qk)
        return
    bpg = INPROJ_GROUP // LANES
    n_groups = len(block_plan) // bpg
    prev = None
    for g in range(n_groups + 1):
        cur = None
        if g < n_groups:
            cur = jnp.dot(h, wqk_ref[:, g * INPROJ_GROUP:(g + 1) * INPROJ_GROUP], preferred_element_type=F32)
        else:
            v_proj()
        if prev is not None:
            epilogue((g - 1) * bpg, prev)
        prev = cur


def _inproj_call(x, mod, mod_row, wqk, wvt, gains, block_plan, rope_tabs, tm, name):
    b, t, d = x.shape
    wq = wqk.shape[1]
    wv = wvt.shape[0]
    rope = rope_tabs is not None
    mod_map = (lambda bi, i: (bi, 0, 0)) if mod_row is None else (lambda bi, i: (mod_row, 0, 0))
    in_specs = [
        pl.BlockSpec((1, tm, d), lambda bi, i: (bi, i, 0)),
        pl.BlockSpec((1,) + mod.shape[1:], mod_map),
        _const_spec(wqk.shape),
        _const_spec(wvt.shape),
        _const_spec(gains.shape),
    ]
    args = [x, mod, wqk, wvt, gains]
    if rope:
        in_specs += [pl.BlockSpec((tm, LANES), lambda bi, i: (i, 0))] * 3
        args += list(rope_tabs)
    return pl.pallas_call(
        functools.partial(_inproj_kernel, block_plan=block_plan, rope=rope),
        grid=(b, t // tm),
        in_specs=in_specs,
        out_specs=[
            pl.BlockSpec((1, tm, wq), lambda bi, i: (bi, i, 0)),
            pl.BlockSpec((1, wv, tm), lambda bi, i: (bi, 0, i)),
        ],
        out_shape=[jax.ShapeDtypeStruct((b, t, wq), BF16), jax.ShapeDtypeStruct((b, wv, t), BF16)],
        compiler_params=_cparams(),
        name=name,
    )(*args)


def _gqa_kernel(*refs, tq, latent_keys):
    if latent_keys:
        q_ref, kc_ref, vc_ref, kl_ref, vl_ref, o_ref = refs
    else:
        q_ref, kc_ref, vc_ref, o_ref = refs
        kl_ref = vl_ref = None
    n_groups = q_ref.shape[2] // (2 * LANES)
    streams = []
    for g in range(n_groups):
        q = q_ref[0, :, g * 2 * LANES:(g + 1) * 2 * LANES]
        parts = _split_heads_rows(q[:, :LANES]) + _split_heads_rows(q[:, LANES:])
        chunks = _key_chunks(kc_ref, vc_ref, kl_ref, vl_ref, slice(g * LANES, (g + 1) * LANES),
                             slice(g * HEAD_DIM, (g + 1) * HEAD_DIM))
        streams.append((jnp.concatenate(parts, axis=0), chunks))
    for g, res in enumerate(_attend(streams)):
        o_ref[0, :, g * 2 * LANES:(g + 1) * 2 * LANES] = jnp.concatenate(
            [_stack_head_rows(res, tq, 0), _stack_head_rows(res, tq, 1)], axis=1).astype(BF16)


def _gqa_call(qk_q, qk_ctx, vt_ctx, qk_lat, vt_lat, tq, groups, name):
    b, t, _ = qk_q.shape
    n_ctx = qk_ctx.shape[1]
    k_blk0 = (GQA_KV_HEADS * 4 * HEAD_DIM) // (groups * LANES)
    latent_keys = qk_lat is not None
    in_specs = [
        pl.BlockSpec((1, tq, groups * 2 * LANES), lambda bi, g, i: (bi, i, g)),
        pl.BlockSpec((1, n_ctx, groups * LANES), lambda bi, g, i: (bi, 0, k_blk0 + g)),
        pl.BlockSpec((1, groups * HEAD_DIM, n_ctx), lambda bi, g, i: (bi, g, 0)),
    ]
    args = [qk_q, qk_ctx, vt_ctx]
    if latent_keys:
        n_lat = qk_lat.shape[1]
        in_specs += [
            pl.BlockSpec((1, n_lat, groups * LANES), lambda bi, g, i: (bi, 0, k_blk0 + g)),
            pl.BlockSpec((1, groups * HEAD_DIM, n_lat), lambda bi, g, i: (bi, g, 0)),
        ]
        args += [qk_lat, vt_lat]
    return pl.pallas_call(
        functools.partial(_gqa_kernel, tq=tq, latent_keys=latent_keys),
        grid=(b, GQA_KV_HEADS // groups, t // tq),
        in_specs=in_specs,
        out_specs=pl.BlockSpec((1, tq, groups * 2 * LANES), lambda bi, g, i: (bi, i, g)),
        out_shape=jax.ShapeDtypeStruct((b, t, GQA_KV_HEADS * 2 * LANES), BF16),
        compiler_params=_cparams(),
        name=name,
    )(*args)


def _diff_kernel(*refs, tq, latent_keys, lambda_init):
    if latent_keys:
        q_ref, kc_ref, vc_ref, kl_ref, vl_ref, lam_ref, g_ref, o_ref = refs
    else:
        q_ref, kc_ref, vc_ref, lam_ref, g_ref, o_ref = refs
        kl_ref = vl_ref = None
    n_heads = q_ref.shape[2] // LANES
    streams = []
    for hd in range(n_heads):
        cols = slice(hd * LANES, (hd + 1) * LANES)
        q_stack = jnp.concatenate(_split_heads_rows(q_ref[0, :, cols]), axis=0)
        streams.append((q_stack, _key_chunks(kc_ref, vc_ref, kl_ref, vl_ref, cols, cols)))
    lv = lam_ref[...]
    lam = (jnp.exp(jnp.sum(lv[0:1] * lv[1:2], axis=1, keepdims=True))
           - jnp.exp(jnp.sum(lv[2:3] * lv[3:4], axis=1, keepdims=True)) + lambda_init)
    for hd, res in enumerate(_attend(streams)):
        o = (res[:, :tq] - lam * res[:, tq:]).T
        o = o * lax.rsqrt(jnp.mean(o * o, axis=-1, keepdims=True) + EPS) * g_ref[...]
        o_ref[0, :, hd * LANES:(hd + 1) * LANES] = (o * (1.0 - lambda_init)).astype(BF16)


def _diff_call(qk_q, qk_ctx, vt_ctx, qk_lat, vt_lat, lam_vec, subln_g, lambda_init, tq, heads, name):
    b, t, wqk = qk_q.shape
    n_ctx = qk_ctx.shape[1]
    n_heads = wqk // (4 * LANES)
    n_steps = n_heads // heads
    q_blk0, k_blk0, v_blk0 = 2 * n_steps, 3 * n_steps, n_steps
    w = heads * LANES
    latent_keys = qk_lat is not None
    in_specs = [
        pl.BlockSpec((1, tq, w), lambda bi, h, i: (bi, i, q_blk0 + h)),
        pl.BlockSpec((1, n_ctx, w), lambda bi, h, i: (bi, 0, k_blk0 + h)),
        pl.BlockSpec((1, w, n_ctx), lambda bi, h, i: (bi, v_blk0 + h, 0)),
    ]
    args = [qk_q, qk_ctx, vt_ctx]
    if latent_keys:
        n_lat = qk_lat.shape[1]
        in_specs += [
            pl.BlockSpec((1, n_lat, w), lambda bi, h, i: (bi, 0, k_blk0 + h)),
            pl.BlockSpec((1, w, n_lat), lambda bi, h, i: (bi, v_blk0 + h, 0)),
        ]
        args += [qk_lat, vt_lat]
    in_specs += [_const_spec(lam_vec.shape), _const_spec((1, LANES))]
    args += [lam_vec, subln_g.reshape(1, LANES)]
    return pl.pallas_call(
        functools.partial(_diff_kernel, tq=tq, latent_keys=latent_keys, lambda_init=lambda_init),
        grid=(b, n_steps, t // tq),
        in_specs=in_specs,
        out_specs=pl.BlockSpec((1, tq, w), lambda bi, h, i: (bi, i, h)),
        out_shape=jax.ShapeDtypeStruct((b, t, n_heads * LANES), BF16),
        compiler_params=_cparams(),
        name=name,
    )(*args)


def _na_kernel(*refs, n_local):
    q_ref, kc_ref, vc_ref = refs[:3]
    k_refs = refs[3:3 + n_local]
    v_refs = refs[3 + n_local:3 + 2 * n_local]
    bias_ref = refs[3 + 2 * n_local] if n_local else None
    o_ref = refs[-1]
    nb, tq = q_ref.shape[0], q_ref.shape[1]
    n_pairs = q_ref.shape[2] // LANES
    streams = []
    for bb in range(nb):
        for p in range(n_pairs):
            cols = slice(p * LANES, (p + 1) * LANES)
            q_stack = jnp.concatenate(_split_heads_rows(q_ref[bb, :, cols]), axis=0)
            chunks = [(kc_ref[bb, :, cols], vc_ref[bb, cols, :], None)]
            for c in range(n_local):
                chunks.append((k_refs[c][bb, :, cols], v_refs[c][bb, cols, :],
                               bias_ref[0, p, c * tq:(c + 1) * tq, :]))
            streams.append((q_stack, chunks))
    for i, res in enumerate(_attend(streams)):
        bb, p = divmod(i, n_pairs)
        o_ref[bb, :, p * LANES:(p + 1) * LANES] = _pick_head_rows(res, tq, 0).astype(BF16)


def _na_call(qk_lat, vt_lat, qk_ctx, vt_ctx, bias_tab, name):
    b, s, _ = qk_lat.shape
    n_ctx = qk_ctx.shape[1]
    tq = NA_ROWS_PER_BLOCK * GRID_W
    nj = s // tq
    n_pairs = bias_tab.shape[1]
    w = n_pairs * LANES
    n_local = 3
    nb = NA_BATCH_PER_STEP if b % NA_BATCH_PER_STEP == 0 else 1

    def tb(j):
        return jnp.clip(j - 1, 0, nj - n_local)

    def cls(j):
        return jnp.where(j == 0, 0, jnp.where(j == nj - 1, 2, 1))

    k_specs = [pl.BlockSpec((nb, tq, w), functools.partial(lambda j, bi, d: (bi, tb(j) + d, 1), d=d))
               for d in range(n_local)]
    v_specs = [pl.BlockSpec((nb, w, tq), functools.partial(lambda j, bi, d: (bi, 0, tb(j) + d), d=d))
               for d in range(n_local)]
    return pl.pallas_call(
        functools.partial(_na_kernel, n_local=n_local),
        grid=(nj, b // nb),
        in_specs=[
            pl.BlockSpec((nb, tq, w), lambda j, bi: (bi, j, 0)),
            pl.BlockSpec((nb, n_ctx, w), lambda j, bi: (bi, 0, 1)),
            pl.BlockSpec((nb, w, n_ctx), lambda j, bi: (bi, 0, 0)),
            *k_specs, *v_specs,
            pl.BlockSpec((1, n_pairs, n_local * tq, 2 * tq), lambda j, bi: (cls(j), 0, 0, 0)),
        ],
        out_specs=pl.BlockSpec((nb, tq, w), lambda j, bi: (bi, j, 0)),
        out_shape=jax.ShapeDtypeStruct((b, s, w), BF16),
        compiler_params=_cparams(),
        name=name,
    )(qk_lat, qk_ctx, vt_ctx, *([qk_lat] * n_local), *([vt_lat] * n_local), bias_tab)


def _na_ctx_call(qk_ctx, vt_ctx, n_pairs, name):
    b, n_ctx, _ = qk_ctx.shape
    w = n_pairs * LANES
    return pl.pallas_call(
        functools.partial(_na_kernel, n_local=0),
        grid=(b,),
        in_specs=[
            pl.BlockSpec((1, n_ctx, w), lambda bi: (bi, 0, 0)),
            pl.BlockSpec((1, n_ctx, w), lambda bi: (bi, 0, 1)),
            pl.BlockSpec((1, w, n_ctx), lambda bi: (bi, 0, 0)),
        ],
        out_specs=pl.BlockSpec((1, n_ctx, w), lambda bi: (bi, 0, 0)),
        out_shape=jax.ShapeDtypeStruct((b, n_ctx, w), BF16),
        compiler_params=_cparams(),
        name=name,
    )(qk_ctx, qk_ctx, vt_ctx)


def _na_bias_tables(rpb, seq):
    n_heads = rpb.shape[0]
    rows = seq // GRID_W
    tq = NA_ROWS_PER_BLOCK * GRID_W
    nj = seq // tq
    kh = min(NA_KH, rows)
    n_dc = rpb.shape[2]
    n_pairs = n_heads // 2
    pad = GRID_W - 1 - (n_dc - 1) // 2
    rpb_pad = jnp.pad(rpb.astype(F32), ((0, 0), (0, 0), (pad, pad)), mode="edge")
    col_tab = jnp.stack([rpb_pad[:, :, GRID_W - 1 - qc:2 * GRID_W - 1 - qc] for qc in range(GRID_W)], axis=-1)
    cols = np.arange(GRID_W)
    cs = np.clip(cols - NA_KW // 2, 0, GRID_W - NA_KW)
    col_ok = (cols[:, None] >= cs[None, :]) & (cols[:, None] < cs[None, :] + NA_KW)
    col_tab = jnp.where(col_ok[None, None], col_tab, -jnp.inf)
    col_tab = col_tab.reshape(n_pairs, 2, col_tab.shape[1], GRID_W, GRID_W).transpose(0, 2, 3, 1, 4)
    masked = jnp.full((n_pairs, GRID_W, 2, GRID_W), -jnp.inf, F32)
    n_kr = 3 * NA_ROWS_PER_BLOCK
    tabs = []
    for j in (0, 1, nj - 1):
        ws = NA_ROWS_PER_BLOCK * int(np.clip(j - 1, 0, nj - 3))
        key_rows = []
        for a in range(n_kr):
            blocks = []
            for qr in range(NA_ROWS_PER_BLOCK):
                kr, r = ws + a, NA_ROWS_PER_BLOCK * j + qr
                rs = int(np.clip(r - kh // 2, 0, rows - kh))
                row_ok = rs <= kr < rs + kh
                blocks.append(col_tab[:, kr - r + NA_KH - 1] if row_ok else masked)
            key_rows.append(jnp.stack(blocks, axis=3))
        tabs.append(jnp.stack(key_rows, axis=1).reshape(n_pairs, 3 * tq, 2 * tq))
    return jnp.stack(tabs)


def _outproj_kernel(*refs, n_att):
    att_refs = refs[:n_att]
    x_ref, mod_ref, w_ref, lng_ref, lnb_ref, o_ref = refs[n_att:]
    gate = mod_ref[0][2:3]
    tm = x_ref.shape[1]
    rb = min(OUTPROJ_ROW_BLOCK, tm)

    def proj(rs):
        y = None
        k0 = 0
        for a_ref in att_refs:
            kw = a_ref.shape[2]
            part = jnp.dot(a_ref[0, rs, :], w_ref[k0:k0 + kw, :], preferred_element_type=F32)
            y = part if y is None else y + part
            k0 += kw
        return y

    prev = None
    for r in range(tm // rb + 1):
        cur = proj(slice(r * rb, (r + 1) * rb)) if r < tm // rb else None
        if prev is not None:
            rs = slice((r - 1) * rb, r * rb)
            z = ALPHA * x_ref[0, rs, :] + gate * prev
            o_ref[0, rs, :] = _layer_norm(z, lng_ref[...], lnb_ref[...])
        prev = cur


def _outproj_call(atts, x, mod, mod_row, w_o, ln_g, ln_b, tm, name):
    b, t, d = x.shape
    mod_map = (lambda bi, i: (bi, 0, 0)) if mod_row is None else (lambda bi, i: (mod_row, 0, 0))
    in_specs = [pl.BlockSpec((1, tm, a.shape[2]), lambda bi, i: (bi, i, 0)) for a in atts]
    in_specs += [
        pl.BlockSpec((1, tm, d), lambda bi, i: (bi, i, 0)),
        pl.BlockSpec((1,) + mod.shape[1:], mod_map),
        _const_spec(w_o.shape),
        _const_spec((1, d)),
        _const_spec((1, d)),
    ]
    return pl.pallas_call(
        functools.partial(_outproj_kernel, n_att=len(atts)),
        grid=(b, t // tm),
        in_specs=in_specs,
        out_specs=pl.BlockSpec((1, tm, d), lambda bi, i: (bi, i, 0)),
        out_shape=jax.ShapeDtypeStruct((b, t, d), F32),
        compiler_params=_cparams(),
        name=name,
    )(*atts, x, mod, w_o, ln_g.reshape(1, d), ln_b.reshape(1, d))


def _ffn_kernel(*refs, tm, halo):
    if halo:
        (x_ref, xp_ref, xn_ref, mod_ref, wup_ref, cw_ref, cb_ref, wdn_ref, lng_ref, lnb_ref,
         o_ref, hs_ref, act_ref) = refs
    else:
        x_ref, mod_ref, wup_ref, cw_ref, cb_ref, wdn_ref, lng_ref, lnb_ref, o_ref, hs_ref, act_ref = refs
    d = x_ref.shape[2]
    d_ff = wdn_ref.shape[0]
    rows = tm + HALO
    mod = mod_ref[0]
    shift, scale, gate = mod[3:4], mod[4:5], mod[5:6]
    x = x_ref[0]
    hs_ref[0:tm, :] = (x * (1.0 + scale) + shift).astype(BF16)
    zeros8 = jnp.zeros((SUBLANES_F32, d), F32)
    if halo:
        i = pl.program_id(1)
        last = pl.num_programs(1) - 1
        hp = jnp.where(i > 0, xp_ref[0] * (1.0 + scale) + shift, 0.0)
        hn = jnp.where(i < last, xn_ref[0] * (1.0 + scale) + shift, 0.0)
    else:
        hp = hn = zeros8
    hs_ref[tm:rows, :] = jnp.concatenate([hn, hp], axis=0).astype(BF16)
    hs = hs_ref[...]
    n_chunks = d_ff // FF_CHUNK

    def up_proj(cols):
        return jnp.dot(hs, wup_ref[:, cols], preferred_element_type=F32)

    def conv(up, cols):
        w = cw_ref[:, cols]
        return (pltpu.roll(up, 1, 0)[0:tm] * w[0:1] + up[0:tm] * w[1:2]
                + pltpu.roll(up, rows - 1, 0)[0:tm] * w[2:3] + cb_ref[:, cols])

    ups = {}
    for t in range(n_chunks + 1):
        if t < n_chunks:
            a_cols = slice(t * FF_CHUNK, (t + 1) * FF_CHUNK)
            g_cols = slice(d_ff + t * FF_CHUNK, d_ff + (t + 1) * FF_CHUNK)
            ups[t] = (up_proj(a_cols), up_proj(g_cols), a_cols, g_cols)
        if t >= 1:
            up_a, up_g, a_cols, g_cols = ups.pop(t - 1)
            act_ref[:, a_cols] = (_gelu_tanh(conv(up_g, g_cols)) * conv(up_a, a_cols)).astype(BF16)
    rb = tm // FFN_OUT_ROW_BLOCKS
    prev = None
    for r in range(FFN_OUT_ROW_BLOCKS + 1):
        cur = None
        if r < FFN_OUT_ROW_BLOCKS:
            cur = jnp.dot(act_ref[r * rb:(r + 1) * rb, :], wdn_ref[...], preferred_element_type=F32)
        if prev is not None:
            rs = slice((r - 1) * rb, r * rb)
            z = ALPHA * x_ref[0, rs, :] + gate * prev
            o_ref[0, rs, :] = _layer_norm(z, lng_ref[...], lnb_ref[...])
        prev = cur


def _ffn_call(x, mod, mod_row, w_up, conv_w, conv_b, w_down, ln_g, ln_b, tm, name):
    b, t, d = x.shape
    halo = t > tm
    mod_map = (lambda bi, i: (bi, 0, 0)) if mod_row is None else (lambda bi, i: (mod_row, 0, 0))
    in_specs = [pl.BlockSpec((1, tm, d), lambda bi, i: (bi, i, 0))]
    args = [x]
    if halo:
        r8 = tm // SUBLANES_F32
        n8 = t // SUBLANES_F32
        in_specs += [
            pl.BlockSpec((1, SUBLANES_F32, d), lambda bi, i: (bi, jnp.maximum(i * r8 - 1, 0), 0)),
            pl.BlockSpec((1, SUBLANES_F32, d), lambda bi, i: (bi, jnp.minimum((i + 1) * r8, n8 - 1), 0)),
        ]
        args += [x, x]
    in_specs += [
        pl.BlockSpec((1,) + mod.shape[1:], mod_map),
        _const_spec(w_up.shape),
        _const_spec(conv_w.shape),
        _const_spec(conv_b.shape),
        _const_spec(w_down.shape),
        _const_spec((1, d)),
        _const_spec((1, d)),
    ]
    args += [mod, w_up, conv_w, conv_b, w_down, ln_g.reshape(1, d), ln_b.reshape(1, d)]
    return pl.pallas_call(
        functools.partial(_ffn_kernel, tm=tm, halo=halo),
        grid=(b, t // tm),
        in_specs=in_specs,
        out_specs=pl.BlockSpec((1, tm, d), lambda bi, i: (bi, i, 0)),
        out_shape=jax.ShapeDtypeStruct((b, t, d), F32),
        scratch_shapes=[pltpu.VMEM((tm + HALO, d), BF16), pltpu.VMEM((tm, w_down.shape[0]), BF16)],
        compiler_params=_cparams(),
        name=name,
    )(*args)


def _rope_tables(seq):
    n_freq = HEAD_DIM // 4
    inv = ROPE_BASE ** (-jnp.arange(n_freq, dtype=F32) / n_freq)
    t = jnp.arange(seq)
    row = (t // GRID_W).astype(F32)
    col = (t % GRID_W).astype(F32)
    ar = row[:, None] * inv
    ac = col[:, None] * inv
    ang = jnp.concatenate([ar, ar, ac, ac], -1)
    cos, sin = jnp.cos(ang), jnp.sin(ang)
    even_chunk = ((np.arange(HEAD_DIM) // n_freq) % 2 == 0)[None, :]
    sin_a = jnp.where(even_chunk, -sin, 0.0)
    sin_b = jnp.where(even_chunk, 0.0, sin)
    return tuple(jnp.concatenate([a, a], axis=1) for a in (cos, sin_a, sin_b))


def _dup_heads(w, n_heads):
    d = w.shape[0]
    w = w.reshape(d, n_heads, HEAD_DIM)
    return jnp.concatenate([w, w], axis=-1).reshape(d, n_heads * PAIR)


def kernel(x, c, ctx, c_ctx, w_ada, b_ada, ln_g, ln_b, w_in_ab, w_o_ab, na_rpb, diff_lambda, diff_subln,
           w_in_c, w_o_c, gqa_qk_norm, w_up, conv_w, conv_b, w_down):
    b, s, d = x.shape
    n_ctx = ctx.shape[1]
    d_ff = w_down.shape[1]
    na_w = na_rpb.shape[1] * HEAD_DIM
    diff_w = d - na_w
    gqa_kw = GQA_KV_HEADS * HEAD_DIM
    tm_lat = min(TM_LATENT, s)

    mod_rows = -(-(b + 1) // SUBLANES_F32) * SUBLANES_F32
    ctx_row = b
    cc = jnp.concatenate([c, c_ctx[None, :], jnp.zeros((mod_rows - b - 1, d), F32)], axis=0)
    mod_all = _ada_call(cc, w_ada, b_ada)
    rope_tabs = _rope_tables(s)
    ones_gain = jnp.ones((2, LANES), F32)

    xc = ctx
    for l in range(DEPTH):
        need_ctx = l < DEPTH - 1
        i = l // 2
        mod = mod_all[l].reshape(mod_rows, 6, d)
        if l % 2 == 0:
            lambda_init = 0.8 - 0.6 * math.exp(-0.3 * l)
            w = w_in_ab[i]
            o0 = 3 * na_w
            wqk = jnp.concatenate([w[:, :2 * na_w], w[:, o0:o0 + 2 * diff_w]], axis=1).astype(BF16)
            wvt = jnp.concatenate([w[:, 2 * na_w:o0], w[:, o0 + 2 * diff_w:]], axis=1).T.astype(BF16)
            nb = na_w // LANES
            plan = ([(None, False, Q_SCALE)] * nb + [(None, False, 1.0)] * nb
                    + [(None, True, Q_SCALE)] * (diff_w // LANES) + [(None, True, 1.0)] * (diff_w // LANES))
            gains = ones_gain
        else:
            w = w_in_c[i]
            q_w = d
            wqk = jnp.concatenate([w[:, :q_w], _dup_heads(w[:, q_w:q_w + gqa_kw], GQA_KV_HEADS)],
                                  axis=1).astype(BF16)
            wvt = w[:, q_w + gqa_kw:].T.astype(BF16)
            plan = [(0, True, Q_SCALE)] * (q_w // LANES) + [(1, True, 1.0)] * GQA_KV_HEADS
            gains = jnp.concatenate([gqa_qk_norm[i], gqa_qk_norm[i]], axis=1)
        plan = tuple(plan)

        qk_lat, vt_lat = _inproj_call(x, mod, None, wqk, wvt, gains, plan, rope_tabs, tm_lat, f"inproj_lat_{l}")
        qk_ctx, vt_ctx = _inproj_call(xc, mod, ctx_row, wqk, wvt, gains, plan, None, n_ctx, f"inproj_ctx_{l}")

        if l % 2 == 0:
            bias_tab = _na_bias_tables(na_rpb[i] * LOG2E, s)
            atts = [
                _na_call(qk_lat, vt_lat, qk_ctx, vt_ctx, bias_tab, f"na_lat_{l}"),
                _diff_call(qk_lat, qk_ctx, vt_ctx, qk_lat, vt_lat, diff_lambda[i], diff_subln[i], lambda_init,
                           min(TQ_DIFF, s), 1, f"diff_lat_{l}"),
            ]
            w_o = w_o_ab[i].astype(BF16)
            if need_ctx:
                atts_c = [
                    _na_ctx_call(qk_ctx, vt_ctx, na_w // LANES, f"na_ctx_{l}"),
                    _diff_call(qk_ctx, qk_ctx, vt_ctx, None, None, diff_lambda[i], diff_subln[i], lambda_init,
                               n_ctx, diff_w // LANES, f"diff_ctx_{l}"),
                ]
        else:
            atts = [_gqa_call(qk_lat, qk_ctx, vt_ctx, qk_lat, vt_lat, min(TQ_GQA, s), 1, f"gqa_lat_{l}")]
            w_o = w_o_c[i].astype(BF16)
            if need_ctx:
                atts_c = [_gqa_call(qk_ctx, qk_ctx, vt_ctx, None, None, n_ctx, GQA_KV_HEADS, f"gqa_ctx_{l}")]

        w_up_l = w_up[l].astype(BF16)
        conv_w_l = conv_w[l]
        conv_b_l = conv_b[l].reshape(1, 2 * d_ff)
        w_down_l = w_down[l].astype(BF16)

        x = _outproj_call(atts, x, mod, None, w_o, ln_g[l, 0], ln_b[l, 0], min(TM_OUTPROJ, s), f"outproj_lat_{l}")
        x = _ffn_call(x, mod, None, w_up_l, conv_w_l, conv_b_l, w_down_l, ln_g[l, 1], ln_b[l, 1],
                      tm_lat, f"ffn_lat_{l}")
        if need_ctx:
            xc = _outproj_call(atts_c, xc, mod, ctx_row, w_o, ln_g[l, 0], ln_b[l, 0], n_ctx, f"outproj_ctx_{l}")
            xc = _ffn_call(xc, mod, ctx_row, w_up_l, conv_w_l, conv_b_l, w_down_l, ln_g[l, 1], ln_b[l, 1],
                           n_ctx, f"ffn_ctx_{l}")
    return x
```

```python
import functools
import math

import numpy as np
import jax
import jax.numpy as jnp
from jax import lax
from jax.experimental import pallas as pl
from jax.experimental.pallas import tpu as pltpu

F32 = jnp.float32
BF16 = jnp.bfloat16

DEPTH = 4
GRID_W = 64
HEAD_DIM = 64
NA_KH = 8
NA_KW = 16
GQA_KV_HEADS = 4
CONV_WIDTH = 3
ROPE_BASE = 10000.0
EPS = 1e-6
ALPHA = (2 * DEPTH) ** 0.25
LOG2E = math.log2(math.e)
Q_SCALE = HEAD_DIM ** -0.5 * LOG2E

LANES = 128
SUBLANES_F32 = 8
SUBLANES_BF16 = 16
MXU_DIM = 256
VMEM_LIMIT_BYTES = 56 * 1024 * 1024

PAIR = 2 * HEAD_DIM
assert PAIR == LANES

TM_LATENT = 512
TM_OUTPROJ = 1024
OUTPROJ_ROW_BLOCK = 256
TQ_GQA = 1024
TQ_DIFF = 2048
NA_ROWS_PER_BLOCK = 4
NA_BATCH_PER_STEP = 2
KEY_CHUNK = MXU_DIM
ONES_ROWS = SUBLANES_BF16
FF_CHUNK = MXU_DIM
ADA_TN = 1536
INPROJ_GROUP = MXU_DIM
HALO = SUBLANES_BF16
FFN_OUT_ROW_BLOCKS = 2


def _cparams():
    return pltpu.CompilerParams(vmem_limit_bytes=VMEM_LIMIT_BYTES)


def _const_spec(shape):
    nd = len(shape)
    return pl.BlockSpec(shape, lambda *_: (0,) * nd, pipeline_mode=pl.Buffered(1))


def _layer_norm(z, g, b):
    mu = jnp.mean(z, axis=-1, keepdims=True)
    d = z - mu
    var = jnp.mean(d * d, axis=-1, keepdims=True)
    return d * lax.rsqrt(var + EPS) * g + b


def _gelu_tanh(x):
    c = math.sqrt(2.0 / math.pi)
    return (0.5 * x) * (1.0 + jnp.tanh(x * (c + (0.044715 * c) * (x * x))))


def _group_sum_matrix():
    r = lax.broadcasted_iota(jnp.int32, (2 * LANES, LANES), 0)
    c = lax.broadcasted_iota(jnp.int32, (2 * LANES, LANES), 1)
    same = ((r % LANES) // HEAD_DIM) == (c // HEAD_DIM)
    return jnp.where(same, 1.0, 0.0).astype(BF16)


def _head_rms(xb, gm, g):
    x2 = xb * xb
    hi = x2.astype(BF16)
    lo = (x2 - hi.astype(F32)).astype(BF16)
    ssq = jnp.dot(jnp.concatenate([hi, lo], axis=1), gm, preferred_element_type=F32)
    return xb * lax.rsqrt(ssq * (1.0 / HEAD_DIM) + EPS) * g


def _rope(xb, cos, sin_a, sin_b):
    return (xb * cos
            + pltpu.roll(xb, LANES - HEAD_DIM // 4, 1) * sin_a
            + pltpu.roll(xb, HEAD_DIM // 4, 1) * sin_b)


def _split_heads_rows(q_pair):
    qf = q_pair.astype(F32)
    lane = lax.broadcasted_iota(jnp.int32, qf.shape, 1)
    lo = lane < HEAD_DIM
    return [jnp.where(lo, qf, 0.0).astype(BF16), jnp.where(lo, 0.0, qf).astype(BF16)]


def _align_heads_rows(q_pair, parity):
    qf = q_pair.astype(F32)
    rolled = pltpu.roll(qf, HEAD_DIM, 1)
    lane = lax.broadcasted_iota(jnp.int32, qf.shape, 1)
    on_key = (lane // HEAD_DIM) == parity
    head_a = jnp.where(parity == 0, qf, rolled)
    head_b = jnp.where(parity == 0, rolled, qf)
    return [jnp.where(on_key, head_a, 0.0).astype(BF16), jnp.where(on_key, head_b, 0.0).astype(BF16)]


def _attend(streams):
    n_chunks = len(streams[0][1])
    s_buf, e_buf = {}, {}
    m = [None] * len(streams)
    acc = [None] * len(streams)
    for t in range(n_chunks + 2):
        for i, (q_stack, chunks) in enumerate(streams):
            if t < n_chunks:
                k_c, _, bias_c = chunks[t]
                s = lax.dot_general(k_c, q_stack, (((1,), (1,)), ((), ())), preferred_element_type=F32)
                s_buf[i, t] = s if bias_c is None else s + bias_c
        for i, (q_stack, chunks) in enumerate(streams):
            if 0 <= t - 1 < n_chunks:
                s = s_buf.pop((i, t - 1))
                mc = jnp.max(s, axis=0, keepdims=True)
                m_new = mc if m[i] is None else jnp.maximum(m[i], mc)
                alpha = None if m[i] is None else jnp.exp2(m[i] - m_new)
                m[i] = m_new
                e_buf[i, t - 1] = (jnp.exp2(s - m_new).astype(BF16), alpha)
        for i, (q_stack, chunks) in enumerate(streams):
            if 0 <= t - 2 < n_chunks:
                e, alpha = e_buf.pop((i, t - 2))
                vt_c = chunks[t - 2][1]
                vt_aug = jnp.concatenate([vt_c, jnp.ones((ONES_ROWS, vt_c.shape[1]), BF16)], axis=0)
                pv = jnp.dot(vt_aug, e, preferred_element_type=F32)
                acc[i] = pv if alpha is None else alpha * acc[i] + pv
    outs = []
    for (q_stack, chunks), a in zip(streams, acc):
        dv = chunks[0][1].shape[0]
        outs.append(a[:dv] * (1.0 / a[dv:dv + 1]))
    return outs


def _key_chunks(kc_ref, vc_ref, kl_ref, vl_ref, k_cols=slice(None), v_rows=slice(None)):
    chunks = [(kc_ref[0, :, k_cols], vc_ref[0, v_rows, :], None)]
    if kl_ref is not None:
        n_lat = kl_ref.shape[1]
        for c0 in range(0, n_lat, KEY_CHUNK):
            chunks.append((kl_ref[0, c0:c0 + KEY_CHUNK, k_cols], vl_ref[0, v_rows, c0:c0 + KEY_CHUNK], None))
    return chunks


def _pick_head_rows(res, tq, p):
    row = lax.broadcasted_iota(jnp.int32, (LANES, tq), 0)
    a = res[:, (2 * p) * tq:(2 * p + 1) * tq]
    b = res[:, (2 * p + 1) * tq:(2 * p + 2) * tq]
    return jnp.where(row < HEAD_DIM, a, b).T


def _stack_head_rows(res, tq, p):
    return jnp.concatenate([res[:, (2 * p) * tq:(2 * p + 1) * tq],
                            res[:, (2 * p + 1) * tq:(2 * p + 2) * tq]], axis=0).T


def _ada_kernel(c_ref, w_ref, b_ref, o_ref):
    c = c_ref[...]
    cond = c * (1.0 / (1.0 + jnp.exp(-c)))
    o_ref[0] = jnp.dot(cond, w_ref[0], precision=lax.Precision.HIGHEST,
                       preferred_element_type=F32) + b_ref[0]


def _ada_call(cc, w_ada, b_ada):
    n_layers, d, n_out = w_ada.shape
    rows = cc.shape[0]
    return pl.pallas_call(
        _ada_kernel,
        grid=(n_layers, n_out // ADA_TN),
        in_specs=[
            pl.BlockSpec((rows, d), lambda l, n: (0, 0)),
            pl.BlockSpec((1, d, ADA_TN), lambda l, n: (l, 0, n)),
            pl.BlockSpec((1, 1, ADA_TN), lambda l, n: (l, 0, n)),
        ],
        out_specs=pl.BlockSpec((1, rows, ADA_TN), lambda l, n: (l, 0, n)),
        out_shape=jax.ShapeDtypeStruct((n_layers, rows, n_out), F32),
        compiler_params=_cparams(),
        name="adaln_mod",
    )(cc, w_ada, b_ada.reshape(n_layers, 1, n_out))


def _inproj_kernel(*refs, block_plan, rope):
    if rope:
        x_ref, mod_ref, wqk_ref, wvt_ref, g_ref, cos_ref, sa_ref, sb_ref, oqk_ref, ovt_ref = refs
    else:
        x_ref, mod_ref, wqk_ref, wvt_ref, g_ref, oqk_ref, ovt_ref = refs
    mod = mod_ref[0]
    h = (x_ref[0] * (1.0 + mod[1:2]) + mod[0:1]).astype(BF16)
    has_norm = any(p[0] is not None for p in block_plan)
    gm = _group_sum_matrix() if has_norm else None
    if rope:
        cos, sa, sb = cos_ref[...], sa_ref[...], sb_ref[...]

    def v_proj():
        vt = lax.dot_general(wvt_ref[...], h, (((1,), (1,)), ((), ())), preferred_element_type=F32)
        ovt_ref[0] = vt.astype(BF16)

    def epilogue(j0, qk):
        for jj in range(qk.shape[1] // LANES):
            gain_row, do_rope, scale = block_plan[j0 + jj]
            blk = qk[:, jj * LANES:(jj + 1) * LANES]
            if gain_row is not None:
                blk = _head_rms(blk, gm, g_ref[gain_row:gain_row + 1, :])
            if do_rope and rope:
                blk = _rope(blk, cos, sa, sb)
            if scale != 1.0:
                blk = blk * scale
            oqk_ref[0, :, (j0 + jj) * LANES:(j0 + jj + 1) * LANES] = blk.astype(BF16)

    if not has_norm:
        qk = jnp.dot(h, wqk_ref[...], preferred_element_type=F32)
        v_proj()
        epilogue(0, qk)
        return
    bpg = INPROJ_GROUP // LANES
    n_groups = len(block_plan) // bpg
    prev = None
    for g in range(n_groups + 1):
        cur = None
        if g < n_groups:
            cur = jnp.dot(h, wqk_ref[:, g * INPROJ_GROUP:(g + 1) * INPROJ_GROUP], preferred_element_type=F32)
        else:
            v_proj()
        if prev is not None:
            epilogue((g - 1) * bpg, prev)
        prev = cur


def _inproj_call(x, mod, mod_row, wqk, wvt, gains, block_plan, rope_tabs, tm, name):
    b, t, d = x.shape
    wq = wqk.shape[1]
    wv = wvt.shape[0]
    rope = rope_tabs is not None
    mod_map = (lambda bi, i: (bi, 0, 0)) if mod_row is None else (lambda bi, i: (mod_row, 0, 0))
    in_specs = [
        pl.BlockSpec((1, tm, d), lambda bi, i: (bi, i, 0)),
        pl.BlockSpec((1,) + mod.shape[1:], mod_map),
        _const_spec(wqk.shape),
        _const_spec(wvt.shape),
        _const_spec(gains.shape),
    ]
    args = [x, mod, wqk, wvt, gains]
    if rope:
        in_specs += [pl.BlockSpec((tm, LANES), lambda bi, i: (i, 0))] * 3
        args += list(rope_tabs)
    return pl.pallas_call(
        functools.partial(_inproj_kernel, block_plan=block_plan, rope=rope),
        grid=(b, t // tm),
        in_specs=in_specs,
        out_specs=[
            pl.BlockSpec((1, tm, wq), lambda bi, i: (bi, i, 0)),
            pl.BlockSpec((1, wv, tm), lambda bi, i: (bi, 0, i)),
        ],
        out_shape=[jax.ShapeDtypeStruct((b, t, wq), BF16), jax.ShapeDtypeStruct((b, wv, t), BF16)],
        compiler_params=_cparams(),
        name=name,
    )(*args)


def _gqa_kernel(*refs, tq, latent_keys):
    if latent_keys:
        q_ref, kc_ref, vc_ref, kl_ref, vl_ref, o_ref = refs
    else:
        q_ref, kc_ref, vc_ref, o_ref = refs
        kl_ref = vl_ref = None
    n_groups = q_ref.shape[2] // (2 * LANES)
    streams = []
    for g in range(n_groups):
        q = q_ref[0, :, g * 2 * LANES:(g + 1) * 2 * LANES]
        if n_groups == 1:
            parity, k_cols = pl.program_id(1) % 2, slice(None)
        else:
            parity, k_cols = g % 2, slice((g // 2) * LANES, (g // 2 + 1) * LANES)
        parts = _align_heads_rows(q[:, :LANES], parity) + _align_heads_rows(q[:, LANES:], parity)
        chunks = _key_chunks(kc_ref, vc_ref, kl_ref, vl_ref, k_cols,
                             slice(g * HEAD_DIM, (g + 1) * HEAD_DIM))
        streams.append((jnp.concatenate(parts, axis=0), chunks))
    for g, res in enumerate(_attend(streams)):
        o_ref[0, :, g * 2 * LANES:(g + 1) * 2 * LANES] = jnp.concatenate(
            [_stack_head_rows(res, tq, 0), _stack_head_rows(res, tq, 1)], axis=1).astype(BF16)


def _gqa_call(qk_q, qk_ctx, vt_ctx, qk_lat, vt_lat, tq, groups, name):
    b, t, _ = qk_q.shape
    n_ctx = qk_ctx.shape[1]
    kw = max(groups * HEAD_DIM, LANES)
    k_blk0 = (GQA_KV_HEADS * 4 * HEAD_DIM) // kw

    def k_map(bi, g, i):
        return (bi, 0, k_blk0 + (g * groups * HEAD_DIM) // kw)

    latent_keys = qk_lat is not None
    in_specs = [
        pl.BlockSpec((1, tq, groups * 2 * LANES), lambda bi, g, i: (bi, i, g)),
        pl.BlockSpec((1, n_ctx, kw), k_map),
        pl.BlockSpec((1, groups * HEAD_DIM, n_ctx), lambda bi, g, i: (bi, g, 0)),
    ]
    args = [qk_q, qk_ctx, vt_ctx]
    if latent_keys:
        n_lat = qk_lat.shape[1]
        in_specs += [
            pl.BlockSpec((1, n_lat, kw), k_map),
            pl.BlockSpec((1, groups * HEAD_DIM, n_lat), lambda bi, g, i: (bi, g, 0)),
        ]
        args += [qk_lat, vt_lat]
    return pl.pallas_call(
        functools.partial(_gqa_kernel, tq=tq, latent_keys=latent_keys),
        grid=(b, GQA_KV_HEADS // groups, t // tq),
        in_specs=in_specs,
        out_specs=pl.BlockSpec((1, tq, groups * 2 * LANES), lambda bi, g, i: (bi, i, g)),
        out_shape=jax.ShapeDtypeStruct((b, t, GQA_KV_HEADS * 2 * LANES), BF16),
        compiler_params=_cparams(),
        name=name,
    )(*args)


def _diff_kernel(*refs, tq, latent_keys, lambda_init):
    if latent_keys:
        q_ref, kc_ref, vc_ref, kl_ref, vl_ref, lam_ref, g_ref, o_ref = refs
    else:
        q_ref, kc_ref, vc_ref, lam_ref, g_ref, o_ref = refs
        kl_ref = vl_ref = None
    n_heads = q_ref.shape[2] // LANES
    streams = []
    for hd in range(n_heads):
        cols = slice(hd * LANES, (hd + 1) * LANES)
        q_stack = jnp.concatenate(_split_heads_rows(q_ref[0, :, cols]), axis=0)
        streams.append((q_stack, _key_chunks(kc_ref, vc_ref, kl_ref, vl_ref, cols, cols)))
    lv = lam_ref[...]
    lam = (jnp.exp(jnp.sum(lv[0:1] * lv[1:2], axis=1, keepdims=True))
           - jnp.exp(jnp.sum(lv[2:3] * lv[3:4], axis=1, keepdims=True)) + lambda_init)
    for hd, res in enumerate(_attend(streams)):
        o = (res[:, :tq] - lam * res[:, tq:]).T
        o = o * lax.rsqrt(jnp.mean(o * o, axis=-1, keepdims=True) + EPS) * g_ref[...]
        o_ref[0, :, hd * LANES:(hd + 1) * LANES] = (o * (1.0 - lambda_init)).astype(BF16)


def _diff_call(qk_q, qk_ctx, vt_ctx, qk_lat, vt_lat, lam_vec, subln_g, lambda_init, tq, heads, name):
    b, t, wqk = qk_q.shape
    n_ctx = qk_ctx.shape[1]
    n_heads = wqk // (4 * LANES)
    n_steps = n_heads // heads
    q_blk0, k_blk0, v_blk0 = 2 * n_steps, 3 * n_steps, n_steps
    w = heads * LANES
    latent_keys = qk_lat is not None
    in_specs = [
        pl.BlockSpec((1, tq, w), lambda bi, h, i: (bi, i, q_blk0 + h)),
        pl.BlockSpec((1, n_ctx, w), lambda bi, h, i: (bi, 0, k_blk0 + h)),
        pl.BlockSpec((1, w, n_ctx), lambda bi, h, i: (bi, v_blk0 + h, 0)),
    ]
    args = [qk_q, qk_ctx, vt_ctx]
    if latent_keys:
        n_lat = qk_lat.shape[1]
        in_specs += [
            pl.BlockSpec((1, n_lat, w), lambda bi, h, i: (bi, 0, k_blk0 + h)),
            pl.BlockSpec((1, w, n_lat), lambda bi, h, i: (bi, v_blk0 + h, 0)),
        ]
        args += [qk_lat, vt_lat]
    in_specs += [_const_spec(lam_vec.shape), _const_spec((1, LANES))]
    args += [lam_vec, subln_g.reshape(1, LANES)]
    return pl.pallas_call(
        functools.partial(_diff_kernel, tq=tq, latent_keys=latent_keys, lambda_init=lambda_init),
        grid=(b, n_steps, t // tq),
        in_specs=in_specs,
        out_specs=pl.BlockSpec((1, tq, w), lambda bi, h, i: (bi, i, h)),
        out_shape=jax.ShapeDtypeStruct((b, t, n_heads * LANES), BF16),
        compiler_params=_cparams(),
        name=name,
    )(*args)


def _na_kernel(*refs, n_local):
    q_ref, kc_ref, vc_ref = refs[:3]
    k_refs = refs[3:3 + n_local]
    v_refs = refs[3 + n_local:3 + 2 * n_local]
    bias_ref = refs[3 + 2 * n_local] if n_local else None
    o_ref = refs[-1]
    nb, tq = q_ref.shape[0], q_ref.shape[1]
    n_pairs = q_ref.shape[2] // LANES
    streams = []
    for bb in range(nb):
        for p in range(n_pairs):
            cols = slice(p * LANES, (p + 1) * LANES)
            q_stack = jnp.concatenate(_split_heads_rows(q_ref[bb, :, cols]), axis=0)
            chunks = [(kc_ref[bb, :, cols], vc_ref[bb, cols, :], None)]
            for c in range(n_local):
                chunks.append((k_refs[c][bb, :, cols], v_refs[c][bb, cols, :],
                               bias_ref[0, p, c * tq:(c + 1) * tq, :]))
            streams.append((q_stack, chunks))
    for i, res in enumerate(_attend(streams)):
        bb, p = divmod(i, n_pairs)
        o_ref[bb, :, p * LANES:(p + 1) * LANES] = _pick_head_rows(res, tq, 0).astype(BF16)


def _na_call(qk_lat, vt_lat, qk_ctx, vt_ctx, bias_tab, name):
    b, s, _ = qk_lat.shape
    n_ctx = qk_ctx.shape[1]
    tq = NA_ROWS_PER_BLOCK * GRID_W
    nj = s // tq
    n_pairs = bias_tab.shape[1]
    w = n_pairs * LANES
    n_local = 3
    nb = NA_BATCH_PER_STEP if b % NA_BATCH_PER_STEP == 0 else 1

    def tb(j):
        return jnp.clip(j - 1, 0, nj - n_local)

    def cls(j):
        return jnp.where(j == 0, 0, jnp.where(j == nj - 1, 2, 1))

    k_specs = [pl.BlockSpec((nb, tq, w), functools.partial(lambda j, bi, d: (bi, tb(j) + d, 1), d=d))
               for d in range(n_local)]
    v_specs = [pl.BlockSpec((nb, w, tq), functools.partial(lambda j, bi, d: (bi, 0, tb(j) + d), d=d))
               for d in range(n_local)]
    return pl.pallas_call(
        functools.partial(_na_kernel, n_local=n_local),
        grid=(nj, b // nb),
        in_specs=[
            pl.BlockSpec((nb, tq, w), lambda j, bi: (bi, j, 0)),
            pl.BlockSpec((nb, n_ctx, w), lambda j, bi: (bi, 0, 1)),
            pl.BlockSpec((nb, w, n_ctx), lambda j, bi: (bi, 0, 0)),
            *k_specs, *v_specs,
            pl.BlockSpec((1, n_pairs, n_local * tq, 2 * tq), lambda j, bi: (cls(j), 0, 0, 0)),
        ],
        out_specs=pl.BlockSpec((nb, tq, w), lambda j, bi: (bi, j, 0)),
        out_shape=jax.ShapeDtypeStruct((b, s, w), BF16),
        compiler_params=_cparams(),
        name=name,
    )(qk_lat, qk_ctx, vt_ctx, *([qk_lat] * n_local), *([vt_lat] * n_local), bias_tab)


def _na_ctx_call(qk_ctx, vt_ctx, n_pairs, name):
    b, n_ctx, _ = qk_ctx.shape
    w = n_pairs * LANES
    return pl.pallas_call(
        functools.partial(_na_kernel, n_local=0),
        grid=(b,),
        in_specs=[
            pl.BlockSpec((1, n_ctx, w), lambda bi: (bi, 0, 0)),
            pl.BlockSpec((1, n_ctx, w), lambda bi: (bi, 0, 1)),
            pl.BlockSpec((1, w, n_ctx), lambda bi: (bi, 0, 0)),
        ],
        out_specs=pl.BlockSpec((1, n_ctx, w), lambda bi: (bi, 0, 0)),
        out_shape=jax.ShapeDtypeStruct((b, n_ctx, w), BF16),
        compiler_params=_cparams(),
        name=name,
    )(qk_ctx, qk_ctx, vt_ctx)


def _na_bias_tables(rpb, seq):
    n_heads = rpb.shape[0]
    rows = seq // GRID_W
    tq = NA_ROWS_PER_BLOCK * GRID_W
    nj = seq // tq
    kh = min(NA_KH, rows)
    n_dc = rpb.shape[2]
    n_pairs = n_heads // 2
    pad = GRID_W - 1 - (n_dc - 1) // 2
    rpb_pad = jnp.pad(rpb.astype(F32), ((0, 0), (0, 0), (pad, pad)), mode="edge")
    col_tab = jnp.stack([rpb_pad[:, :, GRID_W - 1 - qc:2 * GRID_W - 1 - qc] for qc in range(GRID_W)], axis=-1)
    cols = np.arange(GRID_W)
    cs = np.clip(cols - NA_KW // 2, 0, GRID_W - NA_KW)
    col_ok = (cols[:, None] >= cs[None, :]) & (cols[:, None] < cs[None, :] + NA_KW)
    col_tab = jnp.where(col_ok[None, None], col_tab, -jnp.inf)
    col_tab = col_tab.reshape(n_pairs, 2, col_tab.shape[1], GRID_W, GRID_W).transpose(0, 2, 3, 1, 4)
    masked = jnp.full((n_pairs, GRID_W, 2, GRID_W), -jnp.inf, F32)
    n_kr = 3 * NA_ROWS_PER_BLOCK
    tabs = []
    for j in (0, 1, nj - 1):
        ws = NA_ROWS_PER_BLOCK * int(np.clip(j - 1, 0, nj - 3))
        key_rows = []
        for a in range(n_kr):
            blocks = []
            for qr in range(NA_ROWS_PER_BLOCK):
                kr, r = ws + a, NA_ROWS_PER_BLOCK * j + qr
                rs = int(np.clip(r - kh // 2, 0, rows - kh))
                row_ok = rs <= kr < rs + kh
                blocks.append(col_tab[:, kr - r + NA_KH - 1] if row_ok else masked)
            key_rows.append(jnp.stack(blocks, axis=3))
        tabs.append(jnp.stack(key_rows, axis=1).reshape(n_pairs, 3 * tq, 2 * tq))
    return jnp.stack(tabs)


def _outproj_kernel(*refs, n_att):
    att_refs = refs[:n_att]
    x_ref, mod_ref, w_ref, lng_ref, lnb_ref, o_ref = refs[n_att:]
    gate = mod_ref[0][2:3]
    tm = x_ref.shape[1]
    rb = min(OUTPROJ_ROW_BLOCK, tm)

    def proj(rs):
        y = None
        k0 = 0
        for a_ref in att_refs:
            kw = a_ref.shape[2]
            part = jnp.dot(a_ref[0, rs, :], w_ref[k0:k0 + kw, :], preferred_element_type=F32)
            y = part if y is None else y + part
            k0 += kw
        return y

    prev = None
    for r in range(tm // rb + 1):
        cur = proj(slice(r * rb, (r + 1) * rb)) if r < tm // rb else None
        if prev is not None:
            rs = slice((r - 1) * rb, r * rb)
            z = ALPHA * x_ref[0, rs, :] + gate * prev
            o_ref[0, rs, :] = _layer_norm(z, lng_ref[...], lnb_ref[...])
        prev = cur


def _outproj_call(atts, x, mod, mod_row, w_o, ln_g, ln_b, tm, name):
    b, t, d = x.shape
    mod_map = (lambda bi, i: (bi, 0, 0)) if mod_row is None else (lambda bi, i: (mod_row, 0, 0))
    in_specs = [pl.BlockSpec((1, tm, a.shape[2]), lambda bi, i: (bi, i, 0)) for a in atts]
    in_specs += [
        pl.BlockSpec((1, tm, d), lambda bi, i: (bi, i, 0)),
        pl.BlockSpec((1,) + mod.shape[1:], mod_map),
        _const_spec(w_o.shape),
        _const_spec((1, d)),
        _const_spec((1, d)),
    ]
    return pl.pallas_call(
        functools.partial(_outproj_kernel, n_att=len(atts)),
        grid=(b, t // tm),
        in_specs=in_specs,
        out_specs=pl.BlockSpec((1, tm, d), lambda bi, i: (bi, i, 0)),
        out_shape=jax.ShapeDtypeStruct((b, t, d), F32),
        compiler_params=_cparams(),
        name=name,
    )(*atts, x, mod, w_o, ln_g.reshape(1, d), ln_b.reshape(1, d))


def _ffn_kernel(*refs, tm, halo):
    if halo:
        (x_ref, xp_ref, xn_ref, mod_ref, wup_ref, cw_ref, cb_ref, wdn_ref, lng_ref, lnb_ref,
         o_ref, hs_ref, act_ref) = refs
    else:
        x_ref, mod_ref, wup_ref, cw_ref, cb_ref, wdn_ref, lng_ref, lnb_ref, o_ref, hs_ref, act_ref = refs
    d = x_ref.shape[2]
    d_ff = wdn_ref.shape[0]
    rows = tm + HALO
    mod = mod_ref[0]
    shift, scale, gate = mod[3:4], mod[4:5], mod[5:6]
    x = x_ref[0]
    hs_ref[0:tm, :] = (x * (1.0 + scale) + shift).astype(BF16)
    zeros8 = jnp.zeros((SUBLANES_F32, d), F32)
    if halo:
        i = pl.program_id(1)
        last = pl.num_programs(1) - 1
        hp = jnp.where(i > 0, xp_ref[0] * (1.0 + scale) + shift, 0.0)
        hn = jnp.where(i < last, xn_ref[0] * (1.0 + scale) + shift, 0.0)
    else:
        hp = hn = zeros8
    hs_ref[tm:rows, :] = jnp.concatenate([hn, hp], axis=0).astype(BF16)
    hs = hs_ref[...]
    n_chunks = d_ff // FF_CHUNK

    def up_proj(cols):
        return jnp.dot(hs, wup_ref[:, cols], preferred_element_type=F32)

    def conv(up, cols):
        w = cw_ref[:, cols]
        return (pltpu.roll(up, 1, 0)[0:tm] * w[0:1] + up[0:tm] * w[1:2]
                + pltpu.roll(up, rows - 1, 0)[0:tm] * w[2:3] + cb_ref[:, cols])

    ups = {}
    for t in range(n_chunks + 1):
        if t < n_chunks:
            a_cols = slice(t * FF_CHUNK, (t + 1) * FF_CHUNK)
            g_cols = slice(d_ff + t * FF_CHUNK, d_ff + (t + 1) * FF_CHUNK)
            ups[t] = (up_proj(a_cols), up_proj(g_cols), a_cols, g_cols)
        if t >= 1:
            up_a, up_g, a_cols, g_cols = ups.pop(t - 1)
            act_ref[:, a_cols] = (_gelu_tanh(conv(up_g, g_cols)) * conv(up_a, a_cols)).astype(BF16)
    rb = tm // FFN_OUT_ROW_BLOCKS
    prev = None
    for r in range(FFN_OUT_ROW_BLOCKS + 1):
        cur = None
        if r < FFN_OUT_ROW_BLOCKS:
            cur = jnp.dot(act_ref[r * rb:(r + 1) * rb, :], wdn_ref[...], preferred_element_type=F32)
        if prev is not None:
            rs = slice((r - 1) * rb, r * rb)
            z = ALPHA * x_ref[0, rs, :] + gate * prev
            o_ref[0, rs, :] = _layer_norm(z, lng_ref[...], lnb_ref[...])
        prev = cur


def _ffn_call(x, mod, mod_row, w_up, conv_w, conv_b, w_down, ln_g, ln_b, tm, name):
    b, t, d = x.shape
    halo = t > tm
    mod_map = (lambda bi, i: (bi, 0, 0)) if mod_row is None else (lambda bi, i: (mod_row, 0, 0))
    in_specs = [pl.BlockSpec((1, tm, d), lambda bi, i: (bi, i, 0))]
    args = [x]
    if halo:
        r8 = tm // SUBLANES_F32
        n8 = t // SUBLANES_F32
        in_specs += [
            pl.BlockSpec((1, SUBLANES_F32, d), lambda bi, i: (bi, jnp.maximum(i * r8 - 1, 0), 0)),
            pl.BlockSpec((1, SUBLANES_F32, d), lambda bi, i: (bi, jnp.minimum((i + 1) * r8, n8 - 1), 0)),
        ]
        args += [x, x]
    in_specs += [
        pl.BlockSpec((1,) + mod.shape[1:], mod_map),
        _const_spec(w_up.shape),
        _const_spec(conv_w.shape),
        _const_spec(conv_b.shape),
        _const_spec(w_down.shape),
        _const_spec((1, d)),
        _const_spec((1, d)),
    ]
    args += [mod, w_up, conv_w, conv_b, w_down, ln_g.reshape(1, d), ln_b.reshape(1, d)]
    return pl.pallas_call(
        functools.partial(_ffn_kernel, tm=tm, halo=halo),
        grid=(b, t // tm),
        in_specs=in_specs,
        out_specs=pl.BlockSpec((1, tm, d), lambda bi, i: (bi, i, 0)),
        out_shape=jax.ShapeDtypeStruct((b, t, d), F32),
        scratch_shapes=[pltpu.VMEM((tm + HALO, d), BF16), pltpu.VMEM((tm, w_down.shape[0]), BF16)],
        compiler_params=_cparams(),
        name=name,
    )(*args)


def _rope_tables(seq):
    n_freq = HEAD_DIM // 4
    inv = ROPE_BASE ** (-jnp.arange(n_freq, dtype=F32) / n_freq)
    t = jnp.arange(seq)
    row = (t // GRID_W).astype(F32)
    col = (t % GRID_W).astype(F32)
    ar = row[:, None] * inv
    ac = col[:, None] * inv
    ang = jnp.concatenate([ar, ar, ac, ac], -1)
    cos, sin = jnp.cos(ang), jnp.sin(ang)
    even_chunk = ((np.arange(HEAD_DIM) // n_freq) % 2 == 0)[None, :]
    sin_a = jnp.where(even_chunk, -sin, 0.0)
    sin_b = jnp.where(even_chunk, 0.0, sin)
    return tuple(jnp.concatenate([a, a], axis=1) for a in (cos, sin_a, sin_b))


def _dup_heads(w, n_heads):
    d = w.shape[0]
    w = w.reshape(d, n_heads, HEAD_DIM)
    return jnp.concatenate([w, w], axis=-1).reshape(d, n_heads * PAIR)


def kernel(x, c, ctx, c_ctx, w_ada, b_ada, ln_g, ln_b, w_in_ab, w_o_ab, na_rpb, diff_lambda, diff_subln,
           w_in_c, w_o_c, gqa_qk_norm, w_up, conv_w, conv_b, w_down):
    b, s, d = x.shape
    n_ctx = ctx.shape[1]
    d_ff = w_down.shape[1]
    na_w = na_rpb.shape[1] * HEAD_DIM
    diff_w = d - na_w
    gqa_kw = GQA_KV_HEADS * HEAD_DIM
    tm_lat = min(TM_LATENT, s)

    mod_rows = -(-(b + 1) // SUBLANES_F32) * SUBLANES_F32
    ctx_row = b
    cc = jnp.concatenate([c, c_ctx[None, :], jnp.zeros((mod_rows - b - 1, d), F32)], axis=0)
    mod_all = _ada_call(cc, w_ada, b_ada)
    rope_tabs = _rope_tables(s)
    ones_gain = jnp.ones((2, LANES), F32)

    xc = ctx
    for l in range(DEPTH):
        need_ctx = l < DEPTH - 1
        i = l // 2
        mod = mod_all[l].reshape(mod_rows, 6, d)
        if l % 2 == 0:
            lambda_init = 0.8 - 0.6 * math.exp(-0.3 * l)
            w = w_in_ab[i]
            o0 = 3 * na_w
            wqk = jnp.concatenate([w[:, :2 * na_w], w[:, o0:o0 + 2 * diff_w]], axis=1).astype(BF16)
            wvt = jnp.concatenate([w[:, 2 * na_w:o0], w[:, o0 + 2 * diff_w:]], axis=1).T.astype(BF16)
            nb = na_w // LANES
            plan = ([(None, False, Q_SCALE)] * nb + [(None, False, 1.0)] * nb
                    + [(None, True, Q_SCALE)] * (diff_w // LANES) + [(None, True, 1.0)] * (diff_w // LANES))
            gains = ones_gain
        else:
            w = w_in_c[i]
            q_w = d
            wqk = w[:, :q_w + gqa_kw].astype(BF16)
            wvt = w[:, q_w + gqa_kw:].T.astype(BF16)
            plan = [(0, True, Q_SCALE)] * (q_w // LANES) + [(1, True, 1.0)] * (gqa_kw // LANES)
            gains = jnp.concatenate([gqa_qk_norm[i], gqa_qk_norm[i]], axis=1)
        plan = tuple(plan)

        qk_lat, vt_lat = _inproj_call(x, mod, None, wqk, wvt, gains, plan, rope_tabs, tm_lat, f"inproj_lat_{l}")
        qk_ctx, vt_ctx = _inproj_call(xc, mod, ctx_row, wqk, wvt, gains, plan, None, n_ctx, f"inproj_ctx_{l}")

        if l % 2 == 0:
            bias_tab = _na_bias_tables(na_rpb[i] * LOG2E, s)
            atts = [
                _na_call(qk_lat, vt_lat, qk_ctx, vt_ctx, bias_tab, f"na_lat_{l}"),
                _diff_call(qk_lat, qk_ctx, vt_ctx, qk_lat, vt_lat, diff_lambda[i], diff_subln[i], lambda_init,
                           min(TQ_DIFF, s), 1, f"diff_lat_{l}"),
            ]
            w_o = w_o_ab[i].astype(BF16)
            if need_ctx:
                atts_c = [
                    _na_ctx_call(qk_ctx, vt_ctx, na_w // LANES, f"na_ctx_{l}"),
                    _diff_call(qk_ctx, qk_ctx, vt_ctx, None, None, diff_lambda[i], diff_subln[i], lambda_init,
                               n_ctx, diff_w // LANES, f"diff_ctx_{l}"),
                ]
        else:
            atts = [_gqa_call(qk_lat, qk_ctx, vt_ctx, qk_lat, vt_lat, min(TQ_GQA, s), 1, f"gqa_lat_{l}")]
            w_o = w_o_c[i].astype(BF16)
            if need_ctx:
                atts_c = [_gqa_call(qk_ctx, qk_ctx, vt_ctx, None, None, n_ctx, GQA_KV_HEADS, f"gqa_ctx_{l}")]

        w_up_l = w_up[l].astype(BF16)
        conv_w_l = conv_w[l]
        conv_b_l = conv_b[l].reshape(1, 2 * d_ff)
        w_down_l = w_down[l].astype(BF16)

        x = _outproj_call(atts, x, mod, None, w_o, ln_g[l, 0], ln_b[l, 0], min(TM_OUTPROJ, s), f"outproj_lat_{l}")
        x = _ffn_call(x, mod, None, w_up_l, conv_w_l, conv_b_l, w_down_l, ln_g[l, 1], ln_b[l, 1],
                      tm_lat, f"ffn_lat_{l}")
        if need_ctx:
            xc = _outproj_call(atts_c, xc, mod, ctx_row, w_o, ln_g[l, 0], ln_b[l, 0], n_ctx, f"outproj_ctx_{l}")
            xc = _ffn_call(xc, mod, ctx_row, w_up_l, conv_w_l, conv_b_l, w_down_l, ln_g[l, 1], ln_b[l, 1],
                           n_ctx, f"ffn_ctx_{l}")
    return x
```
